```python
import math
import jax, jax.numpy as jnp
from jax import lax
import numpy as np

D_MODEL = 4096
BATCH = 32
SEQ = 256
DEPTH = 2
DEC_BATCH = 4
DEC_SEQ = 1024
PAST_LEN = 512

GRID_W = 64
BRANCH_W = D_MODEL // 2
CHUNK = 128
H_M = 8
DV_M = BRANCH_W // H_M
DK_M = DV_M // 2
SGU_W = BRANCH_W
SGU_GROUPS = 8
H_A = 16
KV_A = 4
HD_A = BRANCH_W // H_A
GQA_G = H_A // KV_A
WINDOW = 128
QBLK = 128
ROPE_BASE = 10000.0
ROPE_AXIS = HD_A // 2
FFN_DIM = 11008
CONV_W = 3
EPS = 1e-6
IN_SIZES = (H_M * DK_M, H_M * DK_M, BRANCH_W, BRANCH_W, 4 * H_M,
            SGU_W, SGU_W, H_A * HD_A, KV_A * HD_A, KV_A * HD_A, 3 * D_MODEL)
SPLIT_IDX = tuple(int(s) for s in np.cumsum(IN_SIZES)[:-1])
N_IN = int(sum(IN_SIZES))

kernel_name = 'hybrid_mlstm_sgu_swa_flow_step'

f32 = jnp.float32


def rmsnorm(x, g):
    xf = x.astype(f32)
    y = xf * lax.rsqrt(jnp.mean(xf * xf, axis=-1, keepdims=True) + EPS)
    return (y * g.astype(f32)).astype(x.dtype)


def axial_rope_tables(T):
    rows = T // GRID_W
    row = jnp.repeat(jnp.arange(rows, dtype=f32), GRID_W)
    col = jnp.tile(jnp.arange(GRID_W, dtype=f32), rows)
    nf = ROPE_AXIS // 2
    inv = ROPE_BASE ** (-jnp.arange(nf, dtype=f32) / nf)
    ar = row[:, None] * inv[None, :]
    ac = col[:, None] * inv[None, :]
    return jnp.cos(ar), jnp.sin(ar), jnp.cos(ac), jnp.sin(ac)


def rotate(x, cos, sin):
    nf = x.shape[-1] // 2
    x1, x2 = x[..., :nf], x[..., nf:]
    c = cos[None, :, None, :]
    s = sin[None, :, None, :]
    return jnp.concatenate([x1 * c - x2 * s, x2 * c + x1 * s], axis=-1)


def rope2d(x, tabs):
    cr, sr, cc, sc = tabs
    xf = x.astype(f32)
    out = jnp.concatenate([rotate(xf[..., :ROPE_AXIS], cr, sr),
                           rotate(xf[..., ROPE_AXIS:], cc, sc)], axis=-1)
    return out.astype(x.dtype)


def mlstm_scan(q, k, v, i_pre, f_pre, init):
    B, T = q.shape[:2]
    nc = T // CHUNK

    def chunks(a):
        a = a.astype(f32).reshape((B, nc, CHUNK) + a.shape[2:])
        return jnp.moveaxis(a, (1, 3), (0, 2))

    xs = (chunks(q) * (DK_M ** -0.5), chunks(k), chunks(v), chunks(i_pre),
          chunks(jax.nn.log_sigmoid(f_pre.astype(f32))))
    tril = jnp.tril(jnp.ones((CHUNK, CHUNK), dtype=bool))

    def step(carry, inp):
        C, n, m = carry
        qc, kc, vc, ic, lf = inp
        b = jnp.cumsum(lf, axis=-1)
        dmat = b[..., :, None] - b[..., None, :] + ic[..., None, :]
        dmat = jnp.where(tril, dmat, -jnp.inf)
        g = b + m[..., None]
        mt = jnp.maximum(g, jnp.max(dmat, axis=-1))
        w_inter = jnp.exp(g - mt)
        a = jnp.exp(dmat - mt[..., None]) * jnp.einsum('bhld,bhsd->bhls', qc, kc)
        num = (w_inter[..., None] * jnp.einsum('bhld,bhde->bhle', qc, C)
               + jnp.einsum('bhls,bhse->bhle', a, vc))
        den = w_inter * jnp.einsum('bhld,bhd->bhl', qc, n) + jnp.sum(a, axis=-1)
        h = num / jnp.maximum(jnp.abs(den), jnp.exp(-mt))[..., None]
        m_new = mt[..., -1]
        w_s = jnp.exp(b[..., -1:] - b + ic - m_new[..., None])
        decay = jnp.exp(g[..., -1] - m_new)
        C_new = decay[..., None, None] * C + jnp.einsum('bhld,bhle->bhde', kc * w_s[..., None], vc)
        n_new = decay[..., None] * n + jnp.einsum('bhl,bhld->bhd', w_s, kc)
        return (C_new, n_new, m_new), h

    C0, n0, m0 = init
    (C, n, m), h = lax.scan(step, (C0.astype(f32), n0.astype(f32), m0.astype(f32)), xs)
    h = jnp.moveaxis(h, (0, 2), (1, 3)).reshape(B, T, H_M, DV_M)
    return h, (C, n, m)


def mlstm_bidir(q, k, v, gp, init_f, init_b):
    flip = lambda a: jnp.flip(a, axis=1)
    h_f, st_f = mlstm_scan(q, k, v, gp[:, :, 0], gp[:, :, 1], init_f)
    h_b, st_b = mlstm_scan(flip(q), flip(k), flip(v), flip(gp[:, :, 2]), flip(gp[:, :, 3]), init_b)
    return h_f + flip(h_b), st_f, st_b


def spatial_gating(u, v, norm_g, w_s, b_s):
    B, T, W = v.shape
    nc = T // CHUNK
    vn = rmsnorm(v, norm_g).reshape(B, nc, CHUNK, SGU_GROUPS, W // SGU_GROUPS)
    mixed = jnp.einsum('gts,bcsgd->bctgd', w_s, vn) + b_s.T[:, :, None]
    return u * mixed.reshape(B, T, W)


def sink_softmax(s, sink):
    sk = jnp.broadcast_to(sink.astype(f32)[:, :, None, None], s.shape[:-1] + (1,))
    return jax.nn.softmax(jnp.concatenate([s, sk], axis=-1), axis=-1)[..., :-1]


def context_attention(q, k, v, sink):
    B, S = q.shape[:2]
    nb = S // QBLK
    qb = jnp.moveaxis(q.reshape(B, nb, QBLK, KV_A, GQA_G, HD_A), 1, 0)

    def block(qi):
        s = jnp.einsum('bqkgd,bpkd->bkgqp', qi, k).astype(f32) * (HD_A ** -0.5)
        p = sink_softmax(s, sink)
        return jnp.einsum('bkgqp,bpkd->bqkgd', p.astype(v.dtype), v)

    o = lax.map(block, qb)
    return jnp.moveaxis(o, 0, 1).reshape(B, S, H_A * HD_A)


def latent_attention(q, k, v, k_ctx, v_ctx, sink):
    B, T = q.shape[:2]
    nb = T // QBLK
    qb = q.reshape(B, nb, QBLK, KV_A, GQA_G, HD_A)

    def band(a):
        ap = jnp.pad(a, ((0, 0), (QBLK, QBLK), (0, 0), (0, 0))).reshape(B, nb + 2, QBLK, KV_A, HD_A)
        return jnp.concatenate([ap[:, :-2], ap[:, 1:-1], ap[:, 2:]], axis=2)

    kb, vb = band(k), band(v)
    j = jnp.arange(nb)[:, None, None]
    a = jnp.arange(QBLK)[None, :, None]
    r = jnp.arange(3 * QBLK)[None, None, :]
    kpos = (j - 1) * QBLK + r
    mask = (jnp.abs(r - QBLK - a) <= WINDOW) & (kpos >= 0) & (kpos < T)
    scale = HD_A ** -0.5
    s_band = jnp.einsum('bnqkgd,bnrkd->bnkgqr', qb, kb).astype(f32) * scale
    s_band = jnp.where(mask[None, :, None, None], s_band, -jnp.inf)
    s_ctx = jnp.einsum('bnqkgd,bpkd->bnkgqp', qb, k_ctx).astype(f32) * scale
    p = sink_softmax(jnp.concatenate([s_band, s_ctx], axis=-1), sink).astype(v.dtype)
    p_band, p_ctx = p[..., :3 * QBLK], p[..., 3 * QBLK:]
    o = (jnp.einsum('bnkgqr,bnrkd->bnqkgd', p_band, vb)
         + jnp.einsum('bnkgqp,bpkd->bnqkgd', p_ctx, v_ctx))
    return o.reshape(B, T, H_A * HD_A)


def conv_ffn(h, w_up, conv_w, conv_b, w_down):
    a = h @ w_up
    a = lax.conv_general_dilated(a, conv_w[:, None, :], window_strides=(1,), padding='SAME',
                                 dimension_numbers=('NWC', 'WIO', 'NWC'),
                                 feature_group_count=a.shape[-1]) + conv_b
    gate, up = jnp.split(a, 2, axis=-1)
    return (jax.nn.silu(gate) * up) @ w_down


def layer(x, mod, p, ctx):
    B, T, _ = x.shape
    sh1, sc1, g1, sh2, sc2, g2 = jnp.split(mod, 6, axis=-1)
    h = rmsnorm(x, p['norm1_g']) * (1 + sc1) + sh1
    mq, mk, mv, mo, mg, su, sv, aq, ak, av, gates = jnp.split(h @ p['w_in'], SPLIT_IDX, axis=-1)
    gp = mg.reshape(B, T, 4, H_M) + p['m_gate_b']
    if ctx is None:
        zero = (jnp.zeros((B, H_M, DK_M, DV_M), f32), jnp.zeros((B, H_M, DK_M), f32),
                jnp.zeros((B, H_M), f32))
        init_f, init_b = zero, zero
    else:
        init_f, init_b = ctx[2], ctx[3]
    hm, st_f, st_b = mlstm_bidir(mq.reshape(B, T, H_M, DK_M), mk.reshape(B, T, H_M, DK_M),
                                 mv.reshape(B, T, H_M, DV_M), gp, init_f, init_b)
    hm = rmsnorm(hm.astype(x.dtype), p['m_norm_g'].reshape(H_M, DV_M)).reshape(B, T, BRANCH_W)
    hm = hm * jax.nn.sigmoid(mo)
    hs = spatial_gating(jax.nn.gelu(su), jax.nn.gelu(sv), p['sgu_norm_g'], p['sgu_w'], p['sgu_b'])
    qa = aq.reshape(B, T, H_A, HD_A)
    ka = ak.reshape(B, T, KV_A, HD_A)
    va = av.reshape(B, T, KV_A, HD_A)
    sink = p['attn_sink'].reshape(KV_A, GQA_G)
    if ctx is None:
        ha = context_attention(qa, ka, va, sink)
    else:
        tabs = axial_rope_tables(T)
        ha = latent_attention(rope2d(qa, tabs), rope2d(ka, tabs), va, ctx[0], ctx[1], sink)
    gm, gs, ga = jnp.split(jax.nn.sigmoid(gates), 3, axis=-1)
    y = gm * (hm @ p['w_br_m']) + gs * (hs @ p['w_br_s']) + ga * (ha @ p['w_br_a'])
    x = x + g1 * (y @ p['w_out'])
    h2 = rmsnorm(x, p['norm2_g']) * (1 + sc2) + sh2
    x = x + g2 * conv_ffn(h2, p['ffn_up'], p['ffn_conv_w'], p['ffn_conv_b'], p['ffn_down'])
    return x, (ka, va, st_f, st_b)


def setup_inputs(seed: int = 0) -> dict:
    key = jax.random.key(seed)
    ks = jax.random.split(key, 32)

    def nrm(i, shape, scale):
        return jax.random.normal(ks[i], shape, jnp.float32) * scale

    L = DEPTH
    F2 = 2 * FFN_DIM
    gate_base = jnp.array([0.0, 3.0, 0.0, 3.0], jnp.float32)[None, :, None]
    return dict(
        x_prompt=nrm(0, (BATCH, SEQ, D_MODEL), 1.0),
        x_sample=nrm(1, (DEC_BATCH, DEC_SEQ, D_MODEL), 1.0),
        cache_k=nrm(2, (DEC_BATCH, L, PAST_LEN, KV_A, HD_A), 1.0),
        cache_v=nrm(3, (DEC_BATCH, L, PAST_LEN, KV_A, HD_A), 1.0),
        state_C=nrm(4, (DEC_BATCH, L, 2, H_M, DK_M, DV_M), 0.3),
        state_n=nrm(5, (DEC_BATCH, L, 2, H_M, DK_M), 0.3),
        state_m=nrm(6, (DEC_BATCH, L, 2, H_M), 1.0),
        c=nrm(7, (DEC_BATCH, D_MODEL), 1.0),
        c_ctx=nrm(8, (D_MODEL,), 1.0),
        ada_w=nrm(9, (L, D_MODEL, 6 * D_MODEL), 0.5 * D_MODEL ** -0.5),
        ada_b=nrm(10, (L, 6 * D_MODEL), 0.02),
        norm1_g=1.0 + nrm(11, (L, D_MODEL), 0.1),
        w_in=nrm(12, (L, D_MODEL, N_IN), D_MODEL ** -0.5),
        m_gate_b=gate_base + nrm(13, (L, 4, H_M), 0.5),
        m_norm_g=1.0 + nrm(14, (L, BRANCH_W), 0.1),
        sgu_norm_g=1.0 + nrm(15, (L, SGU_W), 0.1),
        sgu_w=nrm(16, (L, SGU_GROUPS, CHUNK, CHUNK), CHUNK ** -0.5),
        sgu_b=1.0 + nrm(17, (L, SGU_GROUPS, CHUNK), 0.1),
        attn_sink=nrm(18, (L, H_A), 1.0),
        w_br_m=nrm(19, (L, BRANCH_W, D_MODEL), BRANCH_W ** -0.5),
        w_br_s=nrm(20, (L, SGU_W, D_MODEL), SGU_W ** -0.5),
        w_br_a=nrm(21, (L, H_A * HD_A, D_MODEL), (H_A * HD_A) ** -0.5),
        w_out=nrm(22, (L, D_MODEL, D_MODEL), D_MODEL ** -0.5),
        norm2_g=1.0 + nrm(23, (L, D_MODEL), 0.1),
        ffn_up=nrm(24, (L, D_MODEL, F2), D_MODEL ** -0.5),
        ffn_conv_w=nrm(25, (L, CONV_W, F2), CONV_W ** -0.5),
        ffn_conv_b=nrm(26, (L, F2), 0.02),
        ffn_down=nrm(27, (L, FFN_DIM, D_MODEL), FFN_DIM ** -0.5),
        final_g=1.0 + nrm(28, (D_MODEL,), 0.1),
    )


def reference(x_prompt, x_sample, cache_k, cache_v, state_C, state_n, state_m, c, c_ctx,
              ada_w, ada_b, norm1_g, w_in, m_gate_b, m_norm_g, sgu_norm_g, sgu_w, sgu_b,
              attn_sink, w_br_m, w_br_s, w_br_a, w_out, norm2_g, ffn_up, ffn_conv_w,
              ffn_conv_b, ffn_down, final_g):
    def params(l):
        return dict(norm1_g=norm1_g[l], w_in=w_in[l], m_gate_b=m_gate_b[l], m_norm_g=m_norm_g[l],
                    sgu_norm_g=sgu_norm_g[l], sgu_w=sgu_w[l], sgu_b=sgu_b[l],
                    attn_sink=attn_sink[l], w_br_m=w_br_m[l], w_br_s=w_br_s[l],
                    w_br_a=w_br_a[l], w_out=w_out[l], norm2_g=norm2_g[l], ffn_up=ffn_up[l],
                    ffn_conv_w=ffn_conv_w[l], ffn_conv_b=ffn_conv_b[l], ffn_down=ffn_down[l])

    xp = x_prompt
    ks_l, vs_l, Cs_l, ns_l, ms_l = [], [], [], [], []
    for l in range(DEPTH):
        mod = (jax.nn.silu(c_ctx) @ ada_w[l] + ada_b[l])[None, None, :]
        xp, (k_l, v_l, st_f, st_b) = layer(xp, mod, params(l), None)
        ks_l.append(k_l)
        vs_l.append(v_l)
        Cs_l.append(jnp.stack([st_f[0], st_b[0]], axis=1))
        ns_l.append(jnp.stack([st_f[1], st_b[1]], axis=1))
        ms_l.append(jnp.stack([st_f[2], st_b[2]], axis=1))
    y_prompt = rmsnorm(xp, final_g)
    dt = x_prompt.dtype
    new_cache_k = jnp.stack(ks_l, axis=1)
    new_cache_v = jnp.stack(vs_l, axis=1)
    new_state_C = jnp.stack(Cs_l, axis=1).astype(dt)
    new_state_n = jnp.stack(ns_l, axis=1).astype(dt)
    new_state_m = jnp.stack(ms_l, axis=1).astype(dt)

    xs = x_sample
    for l in range(DEPTH):
        mod = (jax.nn.silu(c) @ ada_w[l] + ada_b[l])[:, None, :]
        ctx = (cache_k[:, l], cache_v[:, l],
               (state_C[:, l, 0], state_n[:, l, 0], state_m[:, l, 0]),
               (state_C[:, l, 1], state_n[:, l, 1], state_m[:, l, 1]))
        xs, _ = layer(xs, mod, params(l), ctx)
    y_sample = rmsnorm(xs, final_g)

    return (y_prompt, y_sample, new_cache_k, new_cache_v, new_state_C, new_state_n, new_state_m)
```

```python
import functools
import math

import numpy as np
import jax
import jax.numpy as jnp
from jax import lax
from jax.experimental import pallas as pl
from jax.experimental.pallas import tpu as pltpu

D_MODEL = 4096
BATCH = 32
SEQ = 256
DEPTH = 2
DEC_BATCH = 4
DEC_SEQ = 1024
PAST_LEN = 512
GRID_W = 64
BRANCH_W = D_MODEL // 2
CHUNK = 128
H_M = 8
DV_M = BRANCH_W // H_M
DK_M = DV_M // 2
SGU_W = BRANCH_W
SGU_GROUPS = 8
SGU_GW = SGU_W // SGU_GROUPS
H_A = 16
KV_A = 4
HD_A = BRANCH_W // H_A
GQA_G = H_A // KV_A
WINDOW = 128
QBLK = 128
ROPE_BASE = 10000.0
ROPE_AXIS = HD_A // 2
FFN_DIM = 11008
EPS = 1e-6

F32 = jnp.float32
BF16 = jnp.bfloat16

M_CTX = BATCH * SEQ
M_LAT = DEC_BATCH * DEC_SEQ
M_TOK = M_CTX + M_LAT
MOD_ROWS = 8
CTX_MOD_ROW = DEC_BATCH

LANES = 128
MIB = 1024 * 1024

OFF_MQ = 0
OFF_MK = OFF_MQ + H_M * DK_M
OFF_MV = OFF_MK + H_M * DK_M
OFF_MO = OFF_MV + BRANCH_W
OFF_SU = OFF_MO + BRANCH_W
OFF_SV = OFF_SU + SGU_W
OFF_AQ = OFF_SV + SGU_W
OFF_AK = OFF_AQ + H_A * HD_A
OFF_AV = OFF_AK + KV_A * HD_A
OFF_GATES = OFF_AV + KV_A * HD_A
N_MAIN = OFF_GATES + 3 * D_MODEL
SRC_MG = OFF_SU
N_MG = 4 * H_M

FFN_PAD = 11264
ROW_TILE = 1024

_GELU_C = float(np.float32(math.sqrt(2.0 / math.pi)))


def _params(sem, vmem_mib):
    return pltpu.CompilerParams(dimension_semantics=sem, vmem_limit_bytes=vmem_mib * MIB)


def _mod_row(i, bm):
    n_ctx = M_CTX // bm
    per_batch = DEC_SEQ // bm
    return jnp.where(i < n_ctx, CTX_MOD_ROW, (i - n_ctx) // per_batch)


def _dot(a, b):
    return jnp.dot(a, b, preferred_element_type=F32)


def _dot_nt(a, b):
    return lax.dot_general(a, b, (((1,), (1,)), ((), ())), preferred_element_type=F32)


def _gelu(x):
    return 0.5 * x * (1.0 + jnp.tanh(_GELU_C * (x + 0.044715 * (x * x * x))))


def _log_sigmoid(x):
    return jnp.minimum(x, 0.0) - jnp.log(1.0 + jnp.exp(-jnp.abs(x)))


def _mod_kernel(c_ref, w_ref, b_ref, o_ref):
    c = c_ref[...]
    s = (c * jax.nn.sigmoid(c)).astype(BF16)
    o_ref[0] = _dot(s, w_ref[0].astype(BF16)) + b_ref[0]


def _modulation(cs, ada_w, ada_b):
    bn = 512
    n = ada_w.shape[-1]
    return pl.pallas_call(
        _mod_kernel,
        grid=(DEPTH, n // bn),
        in_specs=[pl.BlockSpec((MOD_ROWS, D_MODEL), lambda l, j: (0, 0)),
                  pl.BlockSpec((1, D_MODEL, bn), lambda l, j: (l, 0, j)),
                  pl.BlockSpec((1, 1, bn), lambda l, j: (l, 0, j))],
        out_specs=pl.BlockSpec((1, MOD_ROWS, bn), lambda l, j: (l, 0, j)),
        out_shape=jax.ShapeDtypeStruct((DEPTH, MOD_ROWS, n), F32),
        compiler_params=_params(("arbitrary", "arbitrary"), 40),
        name="adaln_mod",
    )(cs, ada_w, ada_b.reshape(DEPTH, 1, n))


def _norm_mod_kernel(x_ref, g_ref, sc_ref, sh_ref, o_ref):
    x = x_ref[...]
    ms = jnp.mean(x * x, axis=-1, keepdims=True)
    y = x * lax.rsqrt(ms + EPS) * g_ref[...]
    o_ref[...] = (y * (1.0 + sc_ref[0]) + sh_ref[0]).astype(o_ref.dtype)


def _norm_mod(x, g, sc, sh):
    bm = 256
    return pl.pallas_call(
        _norm_mod_kernel,
        grid=(M_TOK // bm,),
        in_specs=[pl.BlockSpec((bm, D_MODEL), lambda i: (i, 0)),
                  pl.BlockSpec((1, D_MODEL), lambda i: (0, 0)),
                  pl.BlockSpec((1, 1, D_MODEL), lambda i: (_mod_row(i, bm), 0, 0)),
                  pl.BlockSpec((1, 1, D_MODEL), lambda i: (_mod_row(i, bm), 0, 0))],
        out_specs=pl.BlockSpec((bm, D_MODEL), lambda i: (i, 0)),
        out_shape=jax.ShapeDtypeStruct((M_TOK, D_MODEL), BF16),
        compiler_params=_params(("arbitrary",), 32),
        name="norm_mod",
    )(x, g.reshape(1, D_MODEL), sc, sh)


def _final_norm_kernel(x_ref, g_ref, o_ref):
    x = x_ref[...]
    ms = jnp.mean(x * x, axis=-1, keepdims=True)
    o_ref[...] = x * lax.rsqrt(ms + EPS) * g_ref[...]


def _final_norm(x, g):
    bm = 256
    return pl.pallas_call(
        _final_norm_kernel,
        grid=(M_TOK // bm,),
        in_specs=[pl.BlockSpec((bm, D_MODEL), lambda i: (i, 0)),
                  pl.BlockSpec((1, D_MODEL), lambda i: (0, 0))],
        out_specs=pl.BlockSpec((bm, D_MODEL), lambda i: (i, 0)),
        out_shape=jax.ShapeDtypeStruct((M_TOK, D_MODEL), F32),
        compiler_params=_params(("arbitrary",), 32),
        name="final_norm",
    )(x, g.reshape(1, D_MODEL))


def _mm_kernel(x_ref, w_ref, o_ref):
    o_ref[...] = _dot(x_ref[...], w_ref[...]).astype(o_ref.dtype)


def _matmul(x, w, out_dtype, name):
    m, k = x.shape
    n = w.shape[1]
    bm, bn = ROW_TILE, 1024
    return pl.pallas_call(
        _mm_kernel,
        grid=(n // bn, m // bm),
        in_specs=[pl.BlockSpec((bm, k), lambda j, i: (i, 0)),
                  pl.BlockSpec((k, bn), lambda j, i: (0, j))],
        out_specs=pl.BlockSpec((bm, bn), lambda j, i: (i, j)),
        out_shape=jax.ShapeDtypeStruct((m, n), out_dtype),
        compiler_params=_params(("arbitrary", "arbitrary"), 48),
        name=name,
    )(x, w)


def _gate_kernel(x_ref, w_ref, b_ref, oc_ref, ot_ref):
    g = _dot(x_ref[...], w_ref[...]) + b_ref[...]
    oc_ref[...] = g
    ot_ref[...] = g.T


def _gate_proj(h, w_mg, b_mg):
    bm = ROW_TILE
    return pl.pallas_call(
        _gate_kernel,
        grid=(M_TOK // bm,),
        in_specs=[pl.BlockSpec((bm, D_MODEL), lambda i: (i, 0)),
                  pl.BlockSpec((D_MODEL, LANES), lambda i: (0, 0)),
                  pl.BlockSpec((1, LANES), lambda i: (0, 0))],
        out_specs=[pl.BlockSpec((bm, LANES), lambda i: (i, 0)),
                   pl.BlockSpec((LANES, bm), lambda i: (0, i))],
        out_shape=[jax.ShapeDtypeStruct((M_TOK, LANES), F32),
                   jax.ShapeDtypeStruct((LANES, M_TOK), F32)],
        compiler_params=_params(("arbitrary",), 32),
        name="gate_proj",
    )(h, w_mg, b_mg)


def _mm_res_kernel(x_ref, w_ref, r_ref, g_ref, o_ref):
    o_ref[...] = r_ref[...] + g_ref[0] * _dot(x_ref[...], w_ref[...])


def _matmul_residual(x, w, res, gate):
    m, k = x.shape
    n = w.shape[1]
    bm, bn = ROW_TILE, 1024
    return pl.pallas_call(
        _mm_res_kernel,
        grid=(n // bn, m // bm),
        in_specs=[pl.BlockSpec((bm, k), lambda j, i: (i, 0)),
                  pl.BlockSpec((k, bn), lambda j, i: (0, j)),
                  pl.BlockSpec((bm, bn), lambda j, i: (i, j)),
                  pl.BlockSpec((1, 1, bn), lambda j, i: (_mod_row(i, bm), 0, j))],
        out_specs=pl.BlockSpec((bm, bn), lambda j, i: (i, j)),
        out_shape=jax.ShapeDtypeStruct((m, n), F32),
        compiler_params=_params(("arbitrary", "arbitrary"), 56),
        name="out_proj",
    )(x, w, res, gate)


def _merge_kernel(hm_ref, hs_ref, ha_ref, wm_ref, ws_ref, wa_ref, gm_ref, gs_ref, ga_ref, o_ref):
    y = jax.nn.sigmoid(gm_ref[...]) * _dot(hm_ref[...], wm_ref[...])
    y = y + jax.nn.sigmoid(gs_ref[...]) * _dot(hs_ref[...], ws_ref[...])
    y = y + jax.nn.sigmoid(ga_ref[...]) * _dot(ha_ref[...], wa_ref[...])
    o_ref[...] = y.astype(o_ref.dtype)


def _merge_proj(hm, hs, ha, wm, ws, wa, big):
    bm, bn = 256, 1024
    gate0 = OFF_GATES // bn
    per = D_MODEL // bn
    xspec = pl.BlockSpec((bm, BRANCH_W), lambda j, i: (i, 0))
    wspec = pl.BlockSpec((BRANCH_W, bn), lambda j, i: (0, j))

    def gspec(b):
        return pl.BlockSpec((bm, bn), lambda j, i: (i, gate0 + b * per + j))

    return pl.pallas_call(
        _merge_kernel,
        grid=(D_MODEL // bn, M_TOK // bm),
        in_specs=[xspec, xspec, xspec, wspec, wspec, wspec, gspec(0), gspec(1), gspec(2)],
        out_specs=pl.BlockSpec((bm, bn), lambda j, i: (i, j)),
        out_shape=jax.ShapeDtypeStruct((M_TOK, D_MODEL), BF16),
        compiler_params=_params(("arbitrary", "arbitrary"), 48),
        name="merge_proj",
    )(hm, hs, ha, wm, ws, wa, big, big, big)


def _ffn_up_kernel(x_ref, wg_ref, wu_ref, cwg_ref, cwu_ref, cbg_ref, cbu_ref, o_ref, *, bm):
    i = pl.program_id(1)
    tlen = jnp.where(i < M_CTX // bm, SEQ, DEC_SEQ)
    tpos = lax.broadcasted_iota(jnp.int32, (bm, 1), 0) & (tlen - 1)
    first = tpos == 0
    last = tpos == tlen - 1
    x = x_ref[...]

    def conv(a, cw_ref, cb_ref):
        prev = jnp.where(first, 0.0, pltpu.roll(a, 1, 0))
        nxt = jnp.where(last, 0.0, pltpu.roll(a, bm - 1, 0))
        return cw_ref[0:1, :] * prev + cw_ref[1:2, :] * a + cw_ref[2:3, :] * nxt + cb_ref[...]

    g = conv(_dot(x, wg_ref[...]), cwg_ref, cbg_ref)
    u = conv(_dot(x, wu_ref[...]), cwu_ref, cbu_ref)
    o_ref[...] = (g * jax.nn.sigmoid(g) * u).astype(o_ref.dtype)


def _ffn_up(h2, w_up, conv_w, conv_b):
    bm, bn = ROW_TILE, 512
    nb = FFN_PAD // bn
    return pl.pallas_call(
        functools.partial(_ffn_up_kernel, bm=bm),
        grid=(nb, M_TOK // bm),
        in_specs=[pl.BlockSpec((bm, D_MODEL), lambda j, i: (i, 0)),
                  pl.BlockSpec((D_MODEL, bn), lambda j, i: (0, j)),
                  pl.BlockSpec((D_MODEL, bn), lambda j, i: (0, nb + j)),
                  pl.BlockSpec((3, bn), lambda j, i: (0, j)),
                  pl.BlockSpec((3, bn), lambda j, i: (0, nb + j)),
                  pl.BlockSpec((1, bn), lambda j, i: (0, j)),
                  pl.BlockSpec((1, bn), lambda j, i: (0, nb + j))],
        out_specs=pl.BlockSpec((bm, bn), lambda j, i: (i, j)),
        out_shape=jax.ShapeDtypeStruct((M_TOK, FFN_PAD), BF16),
        compiler_params=_params(("arbitrary", "arbitrary"), 52),
        name="ffn_up",
    )(h2, w_up, w_up, conv_w, conv_w, conv_b, conv_b)


def _ffn_down_kernel(x_ref, w_ref, r_ref, g_ref, o_ref, acc_ref, *, nk):
    k = pl.program_id(2)

    @pl.when(k == 0)
    def _():
        acc_ref[...] = jnp.zeros_like(acc_ref)

    acc_ref[...] += _dot(x_ref[...], w_ref[...])

    @pl.when(k == nk - 1)
    def _():
        o_ref[...] = r_ref[...] + g_ref[0] * acc_ref[...]


def _ffn_down(u, w, res, gate):
    bm, bn, bk = ROW_TILE, 1024, FFN_PAD // 4
    nk = FFN_PAD // bk
    return pl.pallas_call(
        functools.partial(_ffn_down_kernel, nk=nk),
        grid=(D_MODEL // bn, M_TOK // bm, nk),
        in_specs=[pl.BlockSpec((bm, bk), lambda j, i, k: (i, k)),
                  pl.BlockSpec((bk, bn), lambda j, i, k: (k, j)),
                  pl.BlockSpec((bm, bn), lambda j, i, k: (i, j)),
                  pl.BlockSpec((1, 1, bn), lambda j, i, k: (_mod_row(i, bm), 0, j))],
        out_specs=pl.BlockSpec((bm, bn), lambda j, i, k: (i, j)),
        out_shape=jax.ShapeDtypeStruct((M_TOK, D_MODEL), F32),
        scratch_shapes=[pltpu.VMEM((bm, bn), F32)],
        compiler_params=_params(("arbitrary", "arbitrary", "arbitrary"), 52),
        name="ffn_down",
    )(u, w, res, gate)


def _lane_cumsum(x, reverse):
    lane = lax.broadcasted_iota(jnp.int32, x.shape, 1)
    d = 1
    while d < LANES:
        if reverse:
            x = x + jnp.where(lane < LANES - d, pltpu.roll(x, LANES - d, 1), 0.0)
        else:
            x = x + jnp.where(lane >= d, pltpu.roll(x, d, 1), 0.0)
        d *= 2
    return x


def _mlstm_kernel(*refs, seq, with_init, emit_state):
    refs = list(refs)
    q_ref, k_ref, v_ref, mo_ref, gpc_ref, gpt_ref, gain_ref = refs[:7]
    pos = 7
    if with_init:
        c0_ref, n0_ref, m0_ref = refs[pos:pos + 3]
        pos += 3
    hm_ref = refs[pos]
    pos += 1
    if emit_state:
        cf_ref, nf_ref, mf_ref = refs[pos:pos + 3]
        pos += 3
    hf_s, hb_s = refs[pos:pos + 2]

    nc = seq // CHUNK
    head = pl.program_id(1)
    scale = DK_M ** -0.5

    def gate_rows(j):
        sub = lax.broadcasted_iota(jnp.int32, (H_M, CHUNK), 0)
        parts = [jnp.sum(jnp.where(sub == head, gpt_ref[j * H_M:(j + 1) * H_M, c * CHUNK:(c + 1) * CHUNK], 0.0),
                         axis=0, keepdims=True) for c in range(nc)]
        if nc < 8:
            parts.append(jnp.zeros((8 - nc, CHUNK), F32))
        return jnp.concatenate(parts, axis=0)

    i_f = gate_rows(0)
    b_f = _lane_cumsum(_log_sigmoid(gate_rows(1)), reverse=False)
    i_b = gate_rows(2)
    b_b = _lane_cumsum(_log_sigmoid(gate_rows(3)), reverse=True)
    rows = jnp.concatenate([b_f, i_f, b_b, i_b, jnp.zeros((LANES - 32, CHUNK), F32)], axis=0)
    cols = rows.T

    t_idx = lax.broadcasted_iota(jnp.int32, (CHUNK, CHUNK), 0)
    s_idx = lax.broadcasted_iota(jnp.int32, (CHUNK, CHUNK), 1)
    causal = s_idx <= t_idx
    anti = s_idx >= t_idx

    def chunk_step(c, backward, state, h_scr):
        cmat, nvec, mval = state
        r0 = c * CHUNK
        base = 16 if backward else 0
        b_row = rows[base + c:base + c + 1, :]
        i_row = rows[base + 8 + c:base + 9 + c, :]
        b_col = cols[:, base + c:base + c + 1]
        i_col = cols[:, base + 8 + c:base + 9 + c]
        end = 0 if backward else CHUNK - 1

        q = q_ref[r0:r0 + CHUNK, :]
        k = k_ref[r0:r0 + CHUNK, :]
        qb = q.astype(BF16)
        vb = v_ref[r0:r0 + CHUNK, :].astype(BF16)

        dmat = jnp.where(anti if backward else causal, b_col + (i_row - b_row), -jnp.inf)
        g = b_col + mval
        mt = jnp.maximum(g, jnp.max(dmat, axis=1, keepdims=True))
        w_inter = jnp.exp(g - mt)
        a = jnp.exp(dmat - mt) * (_dot_nt(qb, k.astype(BF16)) * scale)
        num = w_inter * (_dot(qb, cmat.astype(BF16)) * scale) + _dot(a.astype(BF16), vb)
        qn = jnp.sum(q * nvec, axis=1, keepdims=True) * scale
        den = w_inter * qn + jnp.sum(a, axis=1, keepdims=True)
        h_scr[r0:r0 + CHUNK, :] = num / jnp.maximum(jnp.abs(den), jnp.exp(-mt))

        m_new = mt[end:end + 1, :]
        w_s = jnp.exp(b_col[end:end + 1, :] - b_col + i_col - m_new)
        decay = jnp.exp(g[end:end + 1, :] - m_new)
        kw = k * w_s
        c_new = decay * cmat + _dot(kw.T.astype(BF16), vb)
        n_new = decay * nvec + jnp.sum(kw, axis=0, keepdims=True)
        return c_new, n_new, m_new

    def init(d):
        if with_init:
            return c0_ref[0, 0, d, 0], n0_ref[0, 0, d, 0], m0_ref[0, 0, d, 0][:, 0:1]
        return jnp.zeros((DK_M, DV_M), F32), jnp.zeros((1, DK_M), F32), jnp.zeros((1, 1), F32)

    st_f, st_b = init(0), init(1)
    for c in range(nc):
        st_f = chunk_step(c, False, st_f, hf_s)
        st_b = chunk_step(nc - 1 - c, True, st_b, hb_s)

    if emit_state:
        for d, st in enumerate((st_f, st_b)):
            cf_ref[0, d, 0] = st[0]
            nf_ref[0, d, 0] = st[1]
            mf_ref[0, d, 0] = jnp.broadcast_to(st[2], (1, LANES))

    gain = gain_ref[0]
    for c in range(nc):
        sl = slice(c * CHUNK, (c + 1) * CHUNK)
        hsum = hf_s[sl, :] + hb_s[sl, :]
        ms = jnp.mean(hsum * hsum, axis=-1, keepdims=True)
        y = hsum * lax.rsqrt(ms + EPS) * gain
        hm_ref[sl, :] = (y * jax.nn.sigmoid(mo_ref[sl, :])).astype(hm_ref.dtype)


def _mlstm(big, gpc, gpt, gain, layer, *, latent, state=None):
    seq = DEC_SEQ if latent else SEQ
    nb = DEC_BATCH if latent else BATCH
    r0 = (M_CTX // seq) if latent else 0
    with_init = latent
    emit_state = not latent

    in_specs = [
        pl.BlockSpec((seq, DK_M), lambda b, h: (r0 + b, OFF_MQ // DK_M + h)),
        pl.BlockSpec((seq, DK_M), lambda b, h: (r0 + b, OFF_MK // DK_M + h)),
        pl.BlockSpec((seq, DV_M), lambda b, h: (r0 + b, OFF_MV // DV_M + h)),
        pl.BlockSpec((seq, DV_M), lambda b, h: (r0 + b, OFF_MO // DV_M + h)),
        pl.BlockSpec((seq, LANES), lambda b, h: (r0 + b, 0)),
        pl.BlockSpec((LANES, seq), lambda b, h: (0, r0 + b)),
        pl.BlockSpec((1, 1, DV_M), lambda b, h: (h, 0, 0)),
    ]
    args = [big, big, big, big, gpc, gpt, gain.reshape(H_M, 1, DV_M)]
    if with_init:
        st_c, st_n, st_m = state
        in_specs += [
            pl.BlockSpec((1, 1, 2, 1, DK_M, DV_M), lambda b, h: (b, layer, 0, h, 0, 0)),
            pl.BlockSpec((1, 1, 2, 1, 1, DK_M), lambda b, h: (b, layer, 0, h, 0, 0)),
            pl.BlockSpec((1, 1, 2, 1, 1, LANES), lambda b, h: (b, layer, 0, h, 0, 0)),
        ]
        args += [st_c, st_n, st_m]
    out_specs = [pl.BlockSpec((seq, DV_M), lambda b, h: (b, h))]
    out_shape = [jax.ShapeDtypeStruct((nb * seq, BRANCH_W), BF16)]
    if emit_state:
        out_specs += [
            pl.BlockSpec((1, 2, 1, DK_M, DV_M), lambda b, h: (b, 0, h, 0, 0)),
            pl.BlockSpec((1, 2, 1, 1, DK_M), lambda b, h: (b, 0, h, 0, 0)),
            pl.BlockSpec((1, 2, 1, 1, LANES), lambda b, h: (b, 0, h, 0, 0)),
        ]
        out_shape += [
            jax.ShapeDtypeStruct((nb, 2, H_M, DK_M, DV_M), F32),
            jax.ShapeDtypeStruct((nb, 2, H_M, 1, DK_M), F32),
            jax.ShapeDtypeStruct((nb, 2, H_M, 1, LANES), F32),
        ]
    return pl.pallas_call(
        functools.partial(_mlstm_kernel, seq=seq, with_init=with_init, emit_state=emit_state),
        grid=(nb, H_M),
        in_specs=in_specs,
        out_specs=out_specs,
        out_shape=out_shape,
        scratch_shapes=[pltpu.VMEM((seq, DV_M), F32), pltpu.VMEM((seq, DV_M), F32)],
        compiler_params=_params(("arbitrary", "arbitrary"), 32),
        name="mlstm_lat" if latent else "mlstm_ctx",
    )(*args)


def _sgu_kernel(su_ref, sv_ref, g_ref, w_ref, b_ref, o_ref):
    v = _gelu(sv_ref[...])
    ms = jnp.mean(v * v, axis=-1, keepdims=True)
    vn = (v * lax.rsqrt(ms + EPS) * g_ref[...]).astype(BF16)
    for g in range(SGU_GROUPS):
        sl = slice(g * SGU_GW, (g + 1) * SGU_GW)
        mixed = _dot(w_ref[g], vn[:, sl]) + b_ref[:, g:g + 1]
        o_ref[:, sl] = (_gelu(su_ref[:, sl]) * mixed).astype(o_ref.dtype)


def _sgu(big, norm_g, w_s, b_s):
    return pl.pallas_call(
        _sgu_kernel,
        grid=(M_TOK // CHUNK,),
        in_specs=[pl.BlockSpec((CHUNK, SGU_W), lambda i: (i, OFF_SU // SGU_W)),
                  pl.BlockSpec((CHUNK, SGU_W), lambda i: (i, OFF_SV // SGU_W)),
                  pl.BlockSpec((1, SGU_W), lambda i: (0, 0)),
                  pl.BlockSpec((SGU_GROUPS, CHUNK, CHUNK), lambda i: (0, 0, 0)),
                  pl.BlockSpec((CHUNK, SGU_GROUPS), lambda i: (0, 0))],
        out_specs=pl.BlockSpec((CHUNK, SGU_W), lambda i: (i, 0)),
        out_shape=jax.ShapeDtypeStruct((M_TOK, SGU_W), BF16),
        compiler_params=_params(("arbitrary",), 32),
        name="sgu",
    )(big, big, norm_g.reshape(1, SGU_W), w_s.astype(BF16), b_s.T)


def _head_sink(sink_ref, idx):
    lane = lax.broadcasted_iota(jnp.int32, (1, LANES), 1)
    return jnp.sum(jnp.where(lane == idx, sink_ref[...], 0.0), axis=1, keepdims=True)


def _ctx_attn_kernel(q_ref, k_ref, v_ref, sink_ref, o_ref):
    kv = pl.program_id(1)
    scale = HD_A ** -0.5
    kb = k_ref[...].astype(BF16)
    vb = v_ref[...].astype(BF16)
    for g in range(GQA_G):
        sl = slice(g * HD_A, (g + 1) * HD_A)
        sk = _head_sink(sink_ref, kv * GQA_G + g)
        s = _dot_nt(q_ref[:, sl].astype(BF16), kb) * scale
        m = jnp.maximum(jnp.max(s, axis=1, keepdims=True), sk)
        p = jnp.exp(s - m)
        den = jnp.sum(p, axis=1, keepdims=True) + jnp.exp(sk - m)
        o_ref[:, sl] = (_dot(p.astype(BF16), vb) / den).astype(o_ref.dtype)


def _ctx_attention(big, sink):
    qw = GQA_G * HD_A
    return pl.pallas_call(
        _ctx_attn_kernel,
        grid=(BATCH, KV_A),
        in_specs=[pl.BlockSpec((SEQ, qw), lambda b, kv: (b, OFF_AQ // qw + kv)),
                  pl.BlockSpec((SEQ, HD_A), lambda b, kv: (b, OFF_AK // HD_A + kv)),
                  pl.BlockSpec((SEQ, HD_A), lambda b, kv: (b, OFF_AV // HD_A + kv)),
                  pl.BlockSpec((1, LANES), lambda b, kv: (0, 0))],
        out_specs=pl.BlockSpec((SEQ, qw), lambda b, kv: (b, kv)),
        out_shape=jax.ShapeDtypeStruct((M_CTX, BRANCH_W), BF16),
        compiler_params=_params(("arbitrary", "arbitrary"), 32),
        name="attn_ctx",
    )(big, big, big, sink)


def _lat_attn_kernel(q_ref, k_ref, v_ref, kc_ref, vc_ref, cos_ref, sin_ref, sink_ref, o_ref, kr_s):
    kv = pl.program_id(1)
    scale = HD_A ** -0.5
    lane = lax.broadcasted_iota(jnp.int32, (1, HD_A), 1)
    first_half = (lane % (ROPE_AXIS)) < (ROPE_AXIS // 2)
    nf = ROPE_AXIS // 2

    def rope(x, sl):
        swapped = jnp.where(first_half, pltpu.roll(x, HD_A - nf, 1), pltpu.roll(x, nf, 1))
        return x * cos_ref[sl, :] + swapped * sin_ref[sl, :]

    kr_s[...] = rope(k_ref[...], slice(None)).astype(BF16)
    kcb = kc_ref[0, 0].astype(BF16)
    vcb = vc_ref[0, 0].astype(BF16)
    nblk = DEC_SEQ // QBLK
    for j in range(nblk):
        lo = max(j - 1, 0) * QBLK
        hi = min(j + 2, nblk) * QBLK
        qs = slice(j * QBLK, (j + 1) * QBLK)
        kband = kr_s[lo:hi, :]
        vband = v_ref[lo:hi, :].astype(BF16)
        qpos = j * QBLK + lax.broadcasted_iota(jnp.int32, (QBLK, hi - lo), 0)
        kpos = lo + lax.broadcasted_iota(jnp.int32, (QBLK, hi - lo), 1)
        mask = jnp.abs(kpos - qpos) <= WINDOW
        for g in range(GQA_G):
            sl = slice(g * HD_A, (g + 1) * HD_A)
            sk = _head_sink(sink_ref, kv * GQA_G + g)
            qg = rope(q_ref[qs, sl], qs).astype(BF16)
            s_band = jnp.where(mask, _dot_nt(qg, kband) * scale, -jnp.inf)
            s_ctx = _dot_nt(qg, kcb) * scale
            m = jnp.maximum(jnp.maximum(jnp.max(s_band, axis=1, keepdims=True),
                                        jnp.max(s_ctx, axis=1, keepdims=True)), sk)
            p_band = jnp.exp(s_band - m)
            p_ctx = jnp.exp(s_ctx - m)
            den = (jnp.sum(p_band, axis=1, keepdims=True) + jnp.sum(p_ctx, axis=1, keepdims=True)
                   + jnp.exp(sk - m))
            o = _dot(p_band.astype(BF16), vband) + _dot(p_ctx.astype(BF16), vcb)
            o_ref[qs, sl] = (o / den).astype(o_ref.dtype)


def _lat_attention(big, cache_k, cache_v, cos_t, sin_t, sink, layer):
    qw = GQA_G * HD_A
    r0 = M_CTX // DEC_SEQ
    return pl.pallas_call(
        _lat_attn_kernel,
        grid=(DEC_BATCH, KV_A),
        in_specs=[pl.BlockSpec((DEC_SEQ, qw), lambda b, kv: (r0 + b, OFF_AQ // qw + kv)),
                  pl.BlockSpec((DEC_SEQ, HD_A), lambda b, kv: (r0 + b, OFF_AK // HD_A + kv)),
                  pl.BlockSpec((DEC_SEQ, HD_A), lambda b, kv: (r0 + b, OFF_AV // HD_A + kv)),
                  pl.BlockSpec((1, 1, PAST_LEN, HD_A), lambda b, kv: (b, layer, 0, kv)),
                  pl.BlockSpec((1, 1, PAST_LEN, HD_A), lambda b, kv: (b, layer, 0, kv)),
                  pl.BlockSpec((DEC_SEQ, HD_A), lambda b, kv: (0, 0)),
                  pl.BlockSpec((DEC_SEQ, HD_A), lambda b, kv: (0, 0)),
                  pl.BlockSpec((1, LANES), lambda b, kv: (0, 0))],
        out_specs=pl.BlockSpec((DEC_SEQ, qw), lambda b, kv: (b, kv)),
        out_shape=jax.ShapeDtypeStruct((M_LAT, BRANCH_W), BF16),
        scratch_shapes=[pltpu.VMEM((DEC_SEQ, HD_A), BF16)],
        compiler_params=_params(("arbitrary", "arbitrary"), 32),
        name="attn_lat",
    )(big, big, big, cache_k, cache_v, cos_t, sin_t, sink)


def _rope_tables():
    rows = DEC_SEQ // GRID_W
    row = jnp.repeat(jnp.arange(rows, dtype=F32), GRID_W)
    col = jnp.tile(jnp.arange(GRID_W, dtype=F32), rows)
    nf = ROPE_AXIS // 2
    inv = ROPE_BASE ** (-jnp.arange(nf, dtype=F32) / nf)
    ar = row[:, None] * inv[None, :]
    ac = col[:, None] * inv[None, :]
    cr, sr, cc, sc = jnp.cos(ar), jnp.sin(ar), jnp.cos(ac), jnp.sin(ac)
    cos_t = jnp.concatenate([cr, cr, cc, cc], axis=1)
    sin_t = jnp.concatenate([-sr, sr, -sc, sc], axis=1)
    return cos_t, sin_t


def _pad_cols(a, width):
    return jnp.pad(a, ((0, 0), (0, width - a.shape[1])))


def _pack_layer(l, w_in, m_gate_b, ffn_up, ffn_conv_w, ffn_conv_b, ffn_down):
    w = w_in[l]
    w_main = jnp.concatenate([w[:, :SRC_MG], w[:, SRC_MG + N_MG:]], axis=1).astype(BF16)
    w_mg = _pad_cols(w[:, SRC_MG:SRC_MG + N_MG], LANES).astype(BF16)
    b_mg = _pad_cols(m_gate_b[l].reshape(1, N_MG), LANES)
    up = ffn_up[l]
    w_up = jnp.concatenate([_pad_cols(up[:, :FFN_DIM], FFN_PAD), _pad_cols(up[:, FFN_DIM:], FFN_PAD)],
                           axis=1).astype(BF16)
    cw = ffn_conv_w[l]
    conv_w = jnp.concatenate([_pad_cols(cw[:, :FFN_DIM], FFN_PAD), _pad_cols(cw[:, FFN_DIM:], FFN_PAD)], axis=1)
    cb = ffn_conv_b[l].reshape(1, -1)
    conv_b = jnp.concatenate([_pad_cols(cb[:, :FFN_DIM], FFN_PAD), _pad_cols(cb[:, FFN_DIM:], FFN_PAD)], axis=1)
    w_down = jnp.pad(ffn_down[l], ((0, FFN_PAD - FFN_DIM), (0, 0))).astype(BF16)
    return w_main, w_mg, b_mg, w_up, conv_w, conv_b, w_down


def kernel(x_prompt, x_sample, cache_k, cache_v, state_C, state_n, state_m, c, c_ctx, ada_w, ada_b, norm1_g, w_in, m_gate_b, m_norm_g, sgu_norm_g, sgu_w, sgu_b, attn_sink, w_br_m, w_br_s, w_br_a, w_out, norm2_g, ffn_up, ffn_conv_w, ffn_conv_b, ffn_down, final_g):
    x = jnp.concatenate([x_prompt.reshape(M_CTX, D_MODEL), x_sample.reshape(M_LAT, D_MODEL)], axis=0)

    cs = jnp.concatenate([c, c_ctx[None, :], jnp.zeros((MOD_ROWS - DEC_BATCH - 1, D_MODEL), F32)], axis=0)
    mod = _modulation(cs, ada_w, ada_b)

    cos_t, sin_t = _rope_tables()
    cache_k2 = cache_k.reshape(DEC_BATCH, DEPTH, PAST_LEN, KV_A * HD_A)
    cache_v2 = cache_v.reshape(DEC_BATCH, DEPTH, PAST_LEN, KV_A * HD_A)
    st_n = state_n.reshape(DEC_BATCH, DEPTH, 2, H_M, 1, DK_M)
    st_m = jnp.broadcast_to(state_m[..., None, None], (DEC_BATCH, DEPTH, 2, H_M, 1, LANES))

    ks, vs, cfs, nfs, mfs = [], [], [], [], []
    for l in range(DEPTH):
        w_main, w_mg, b_mg, w_up, conv_w, conv_b, w_down = _pack_layer(
            l, w_in, m_gate_b, ffn_up, ffn_conv_w, ffn_conv_b, ffn_down)
        sh1, sc1, g1, sh2, sc2, g2 = [mod[l, :, j * D_MODEL:(j + 1) * D_MODEL].reshape(MOD_ROWS, 1, D_MODEL)
                                      for j in range(6)]
        sink = _pad_cols(attn_sink[l].reshape(1, H_A), LANES)

        h = _norm_mod(x, norm1_g[l], sc1, sh1)
        big = _matmul(h, w_main, F32, "in_proj")
        gpc, gpt = _gate_proj(h, w_mg, b_mg)

        hm_c, cf, nf, mf = _mlstm(big, gpc, gpt, m_norm_g[l], l, latent=False)
        (hm_l,) = _mlstm(big, gpc, gpt, m_norm_g[l], l, latent=True, state=(state_C, st_n, st_m))
        hs = _sgu(big, sgu_norm_g[l], sgu_w[l], sgu_b[l])
        ha_c = _ctx_attention(big, sink)
        ha_l = _lat_attention(big, cache_k2, cache_v2, cos_t, sin_t, sink, l)
        hm = jnp.concatenate([hm_c, hm_l], axis=0)
        ha = jnp.concatenate([ha_c, ha_l], axis=0)

        y = _merge_proj(hm, hs, ha, w_br_m[l].astype(BF16), w_br_s[l].astype(BF16), w_br_a[l].astype(BF16), big)
        x = _matmul_residual(y, w_out[l].astype(BF16), x, g1)

        h2 = _norm_mod(x, norm2_g[l], sc2, sh2)
        u = _ffn_up(h2, w_up, conv_w, conv_b)
        x = _ffn_down(u, w_down, x, g2)

        ks.append(big[:M_CTX, OFF_AK:OFF_AK + KV_A * HD_A].reshape(BATCH, SEQ, KV_A, HD_A))
        vs.append(big[:M_CTX, OFF_AV:OFF_AV + KV_A * HD_A].reshape(BATCH, SEQ, KV_A, HD_A))
        cfs.append(cf)
        nfs.append(nf.reshape(BATCH, 2, H_M, DK_M))
        mfs.append(mf[:, :, :, 0, 0])

    y = _final_norm(x, final_g)
    y_prompt = y[:M_CTX].reshape(BATCH, SEQ, D_MODEL)
    y_sample = y[M_CTX:].reshape(DEC_BATCH, DEC_SEQ, D_MODEL)
    return (y_prompt, y_sample, jnp.stack(ks, axis=1), jnp.stack(vs, axis=1),
            jnp.stack(cfs, axis=1), jnp.stack(nfs, axis=1), jnp.stack(mfs, axis=1))
```

```python
import functools
import math

import numpy as np
import jax
import jax.numpy as jnp
from jax import lax
from jax.experimental import pallas as pl
from jax.experimental.pallas import tpu as pltpu

D_MODEL = 4096
BATCH = 32
SEQ = 256
DEPTH = 2
DEC_BATCH = 4
DEC_SEQ = 1024
PAST_LEN = 512
GRID_W = 64
BRANCH_W = D_MODEL // 2
CHUNK = 128
H_M = 8
DV_M = BRANCH_W // H_M
DK_M = DV_M // 2
SGU_W = BRANCH_W
SGU_GROUPS = 8
SGU_GW = SGU_W // SGU_GROUPS
H_A = 16
KV_A = 4
HD_A = BRANCH_W // H_A
GQA_G = H_A // KV_A
WINDOW = 128
QBLK = 128
ROPE_BASE = 10000.0
ROPE_AXIS = HD_A // 2
FFN_DIM = 11008
EPS = 1e-6

F32 = jnp.float32
BF16 = jnp.bfloat16

M_CTX = BATCH * SEQ
M_LAT = DEC_BATCH * DEC_SEQ
M_TOK = M_CTX + M_LAT
MOD_ROWS = 8
CTX_MOD_ROW = DEC_BATCH

LANES = 128
MIB = 1024 * 1024

N_MG = 4 * H_M
OFF_MQ = 0
OFF_MK = OFF_MQ + H_M * DK_M
OFF_MV = OFF_MK + H_M * DK_M
OFF_MO = OFF_MV + BRANCH_W
N_A = OFF_MO + BRANCH_W
OFF_SU = 0
OFF_SV = OFF_SU + SGU_W
OFF_AQ = OFF_SV + SGU_W
OFF_AK = OFF_AQ + H_A * HD_A
OFF_AV = OFF_AK + KV_A * HD_A
OFF_GATES = OFF_AV + KV_A * HD_A
N_B = OFF_GATES + 3 * D_MODEL

FFN_TILE = 256
FFN_PAD = 11264
FFN_REAL_TILES = FFN_DIM // FFN_TILE
ROW_TILE = 1024
W_TILE = 512

_GELU_C = float(np.float32(math.sqrt(2.0 / math.pi)))


def _params(sem, vmem_mib):
    return pltpu.CompilerParams(dimension_semantics=sem, vmem_limit_bytes=vmem_mib * MIB)


def _mod_row(i, bm):
    n_ctx = M_CTX // bm
    per_batch = DEC_SEQ // bm
    return jnp.where(i < n_ctx, CTX_MOD_ROW, (i - n_ctx) // per_batch)


def _ctx_tile(i, bm):
    return jnp.minimum(i, M_CTX // bm - 1)


def _lat_tile(i, bm):
    return jnp.maximum(i - M_CTX // bm, 0)


def _dot(a, b):
    return jnp.dot(a, b, preferred_element_type=F32)


def _dot_nt(a, b):
    return lax.dot_general(a, b, (((1,), (1,)), ((), ())), preferred_element_type=F32)


def _gelu(x):
    return 0.5 * x * (1.0 + jnp.tanh(_GELU_C * (x + 0.044715 * (x * x * x))))


def _log_sigmoid(x):
    return jnp.minimum(x, 0.0) - jnp.log(1.0 + jnp.exp(-jnp.abs(x)))


def _mod_kernel(c_ref, w_ref, b_ref, o_ref):
    c = c_ref[...]
    s = (c * jax.nn.sigmoid(c)).astype(BF16)
    o_ref[0] = _dot(s, w_ref[0].astype(BF16)) + b_ref[0]


def _modulation(cs, ada_w, ada_b):
    bn = 512
    n = ada_w.shape[-1]
    return pl.pallas_call(
        _mod_kernel,
        grid=(DEPTH, n // bn),
        in_specs=[pl.BlockSpec((MOD_ROWS, D_MODEL), lambda l, j: (0, 0)),
                  pl.BlockSpec((1, D_MODEL, bn), lambda l, j: (l, 0, j)),
                  pl.BlockSpec((1, 1, bn), lambda l, j: (l, 0, j))],
        out_specs=pl.BlockSpec((1, MOD_ROWS, bn), lambda l, j: (l, 0, j)),
        out_shape=jax.ShapeDtypeStruct((DEPTH, MOD_ROWS, n), F32),
        compiler_params=_params(("arbitrary", "arbitrary"), 40),
        name="adaln_mod",
    )(cs, ada_w, ada_b.reshape(DEPTH, 1, n))


def _norm_mod_body(x, g_ref, sc_ref, sh_ref, o_ref):
    ms = jnp.mean(x * x, axis=-1, keepdims=True)
    y = x * lax.rsqrt(ms + EPS) * g_ref[...]
    o_ref[...] = (y * (1.0 + sc_ref[0]) + sh_ref[0]).astype(o_ref.dtype)


def _norm_mod_kernel(x_ref, g_ref, sc_ref, sh_ref, o_ref):
    _norm_mod_body(x_ref[...], g_ref, sc_ref, sh_ref, o_ref)


def _norm_mod_split_kernel(xc_ref, xl_ref, g_ref, sc_ref, sh_ref, o_ref, *, bm):
    i = pl.program_id(0)

    @pl.when(i < M_CTX // bm)
    def _():
        _norm_mod_body(xc_ref[...], g_ref, sc_ref, sh_ref, o_ref)

    @pl.when(i >= M_CTX // bm)
    def _():
        _norm_mod_body(xl_ref[...], g_ref, sc_ref, sh_ref, o_ref)


def _norm_mod(xs, g, sc, sh):
    bm = 256
    mod_spec = pl.BlockSpec((1, 1, D_MODEL), lambda i: (_mod_row(i, bm), 0, 0))
    if len(xs) == 1:
        kern = _norm_mod_kernel
        x_specs = [pl.BlockSpec((bm, D_MODEL), lambda i: (i, 0))]
    else:
        kern = functools.partial(_norm_mod_split_kernel, bm=bm)
        x_specs = [pl.BlockSpec((bm, D_MODEL), lambda i: (_ctx_tile(i, bm), 0)),
                   pl.BlockSpec((bm, D_MODEL), lambda i: (_lat_tile(i, bm), 0))]
    return pl.pallas_call(
        kern,
        grid=(M_TOK // bm,),
        in_specs=x_specs + [pl.BlockSpec((1, D_MODEL), lambda i: (0, 0)), mod_spec, mod_spec],
        out_specs=pl.BlockSpec((bm, D_MODEL), lambda i: (i, 0)),
        out_shape=jax.ShapeDtypeStruct((M_TOK, D_MODEL), BF16),
        compiler_params=_params(("arbitrary",), 32),
        name="norm_mod",
    )(*xs, g.reshape(1, D_MODEL), sc, sh)


def _final_norm_kernel(x_ref, g_ref, o_ref):
    x = x_ref[...]
    ms = jnp.mean(x * x, axis=-1, keepdims=True)
    o_ref[...] = x * lax.rsqrt(ms + EPS) * g_ref[...]


def _final_norm(x, g, row0, rows):
    bm = 256
    b0 = row0 // bm
    return pl.pallas_call(
        _final_norm_kernel,
        grid=(rows // bm,),
        in_specs=[pl.BlockSpec((bm, D_MODEL), lambda i: (b0 + i, 0)),
                  pl.BlockSpec((1, D_MODEL), lambda i: (0, 0))],
        out_specs=pl.BlockSpec((bm, D_MODEL), lambda i: (i, 0)),
        out_shape=jax.ShapeDtypeStruct((rows, D_MODEL), F32),
        compiler_params=_params(("arbitrary",), 32),
        name="final_norm",
    )(x, g.reshape(1, D_MODEL))


def _mm_cast_kernel(x_ref, w_ref, o_ref, wb_ref):
    @pl.when(pl.program_id(1) == 0)
    def _():
        wb_ref[...] = w_ref[0].astype(BF16)

    o_ref[...] = _dot(x_ref[...], wb_ref[...]).astype(o_ref.dtype)


def _in_proj_a(h, w_in, layer):
    bm, bn = ROW_TILE, W_TILE
    return pl.pallas_call(
        _mm_cast_kernel,
        grid=(N_A // bn, M_TOK // bm),
        in_specs=[pl.BlockSpec((bm, D_MODEL), lambda j, i: (i, 0)),
                  pl.BlockSpec((1, D_MODEL, bn), lambda j, i: (layer, 0, j))],
        out_specs=pl.BlockSpec((bm, bn), lambda j, i: (i, j)),
        out_shape=jax.ShapeDtypeStruct((M_TOK, N_A), BF16),
        scratch_shapes=[pltpu.VMEM((D_MODEL, bn), BF16)],
        compiler_params=_params(("arbitrary", "arbitrary"), 48),
        name="in_proj_a",
    )(h, w_in)


def _in_proj_b_kernel(x_ref, wa_ref, wn_ref, o_ref, wb_ref, *, bn):
    rc = 256
    shift = LANES - N_MG

    @pl.when(pl.program_id(1) == 0)
    def _():
        keep = lax.broadcasted_iota(jnp.int32, (rc, LANES), 1) < shift

        def body(r, carry):
            rows = pl.ds(pl.multiple_of(r * rc, rc), rc)
            groups = [wa_ref[0, rows, g * LANES:(g + 1) * LANES] for g in range(bn // LANES)]
            groups.append(wn_ref[0, rows, :])
            rolled = [pltpu.roll(t, shift, 1) for t in groups]
            for g in range(bn // LANES):
                wb_ref[rows, g * LANES:(g + 1) * LANES] = jnp.where(keep, rolled[g], rolled[g + 1]).astype(BF16)
            return carry

        lax.fori_loop(0, D_MODEL // rc, body, 0)

    o_ref[...] = _dot(x_ref[...], wb_ref[...]).astype(o_ref.dtype)


def _in_proj_b(h, w_in, layer):
    bm, bn = ROW_TILE, W_TILE
    a0 = N_A // bn
    n0 = N_A // LANES
    per = bn // LANES
    return pl.pallas_call(
        functools.partial(_in_proj_b_kernel, bn=bn),
        grid=(N_B // bn, M_TOK // bm),
        in_specs=[pl.BlockSpec((bm, D_MODEL), lambda j, i: (i, 0)),
                  pl.BlockSpec((1, D_MODEL, bn), lambda j, i: (layer, 0, a0 + j)),
                  pl.BlockSpec((1, D_MODEL, LANES), lambda j, i: (layer, 0, n0 + per * (j + 1)))],
        out_specs=pl.BlockSpec((bm, bn), lambda j, i: (i, j)),
        out_shape=jax.ShapeDtypeStruct((M_TOK, N_B), BF16),
        scratch_shapes=[pltpu.VMEM((D_MODEL, bn), BF16)],
        compiler_params=_params(("arbitrary", "arbitrary"), 48),
        name="in_proj_b",
    )(h, w_in, w_in)


def _gate_kernel(x_ref, w_ref, b_ref, oc_ref, ot_ref):
    g = _dot(x_ref[...], w_ref[0].astype(BF16)) + b_ref[...]
    oc_ref[...] = g
    ot_ref[...] = g.T


def _gate_proj(h, w_in, b_mg, layer):
    bm = ROW_TILE
    return pl.pallas_call(
        _gate_kernel,
        grid=(M_TOK // bm,),
        in_specs=[pl.BlockSpec((bm, D_MODEL), lambda i: (i, 0)),
                  pl.BlockSpec((1, D_MODEL, LANES), lambda i: (layer, 0, N_A // LANES)),
                  pl.BlockSpec((1, LANES), lambda i: (0, 0))],
        out_specs=[pl.BlockSpec((bm, LANES), lambda i: (i, 0)),
                   pl.BlockSpec((LANES, bm), lambda i: (0, i))],
        out_shape=[jax.ShapeDtypeStruct((M_TOK, LANES), F32),
                   jax.ShapeDtypeStruct((LANES, M_TOK), F32)],
        compiler_params=_params(("arbitrary",), 32),
        name="gate_proj",
    )(h, w_in, b_mg)


def _out_proj_kernel(*refs, bm, split):
    if split:
        x_ref, w_ref, rc_ref, rl_ref, g_ref, o_ref, wb_ref = refs
    else:
        x_ref, w_ref, r_ref, g_ref, o_ref, wb_ref = refs
    i = pl.program_id(1)

    @pl.when(i == 0)
    def _():
        wb_ref[...] = w_ref[0].astype(BF16)

    y = g_ref[0] * _dot(x_ref[...], wb_ref[...])
    if split:
        @pl.when(i < M_CTX // bm)
        def _():
            o_ref[...] = rc_ref[...] + y

        @pl.when(i >= M_CTX // bm)
        def _():
            o_ref[...] = rl_ref[...] + y
    else:
        o_ref[...] = r_ref[...] + y


def _out_proj(y, w_out, layer, res, gate):
    bm, bn = ROW_TILE, W_TILE
    split = len(res) == 2
    if split:
        r_specs = [pl.BlockSpec((bm, bn), lambda j, i: (_ctx_tile(i, bm), j)),
                   pl.BlockSpec((bm, bn), lambda j, i: (_lat_tile(i, bm), j))]
    else:
        r_specs = [pl.BlockSpec((bm, bn), lambda j, i: (i, j))]
    return pl.pallas_call(
        functools.partial(_out_proj_kernel, bm=bm, split=split),
        grid=(D_MODEL // bn, M_TOK // bm),
        in_specs=[pl.BlockSpec((bm, D_MODEL), lambda j, i: (i, 0)),
                  pl.BlockSpec((1, D_MODEL, bn), lambda j, i: (layer, 0, j))] + r_specs + [
                  pl.BlockSpec((1, 1, bn), lambda j, i: (_mod_row(i, bm), 0, j))],
        out_specs=pl.BlockSpec((bm, bn), lambda j, i: (i, j)),
        out_shape=jax.ShapeDtypeStruct((M_TOK, D_MODEL), F32),
        scratch_shapes=[pltpu.VMEM((D_MODEL, bn), BF16)],
        compiler_params=_params(("arbitrary", "arbitrary"), 52),
        name="out_proj",
    )(y, w_out, *res, gate)


def _merge_kernel(hmc_ref, hml_ref, hs_ref, hac_ref, hal_ref, wm_ref, ws_ref, wa_ref,
                  gm_ref, gs_ref, ga_ref, o_ref, wmb_ref, wsb_ref, wab_ref, *, bm):
    i = pl.program_id(1)

    @pl.when(i == 0)
    def _():
        wmb_ref[...] = wm_ref[0].astype(BF16)
        wsb_ref[...] = ws_ref[0].astype(BF16)
        wab_ref[...] = wa_ref[0].astype(BF16)

    def gate(ref):
        return jax.nn.sigmoid(ref[...].astype(F32))

    def finish(hm, ha):
        y = gate(gm_ref) * _dot(hm, wmb_ref[...])
        y = y + gate(gs_ref) * _dot(hs_ref[...], wsb_ref[...])
        y = y + gate(ga_ref) * _dot(ha, wab_ref[...])
        o_ref[...] = y.astype(o_ref.dtype)

    @pl.when(i < M_CTX // bm)
    def _():
        finish(hmc_ref[...], hac_ref[...])

    @pl.when(i >= M_CTX // bm)
    def _():
        finish(hml_ref[...], hal_ref[...])


def _merge_proj(hm_c, hm_l, hs, ha_c, ha_l, wm, ws, wa, big_b, layer):
    bm, bn = 256, 1024
    gate0 = OFF_GATES // bn
    per = D_MODEL // bn
    cspec = pl.BlockSpec((bm, BRANCH_W), lambda j, i: (_ctx_tile(i, bm), 0))
    lspec = pl.BlockSpec((bm, BRANCH_W), lambda j, i: (_lat_tile(i, bm), 0))
    xspec = pl.BlockSpec((bm, BRANCH_W), lambda j, i: (i, 0))
    wspec = pl.BlockSpec((1, BRANCH_W, bn), lambda j, i: (layer, 0, j), pipeline_mode=pl.Buffered(1))

    def gspec(b):
        return pl.BlockSpec((bm, bn), lambda j, i: (i, gate0 + b * per + j))

    return pl.pallas_call(
        functools.partial(_merge_kernel, bm=bm),
        grid=(D_MODEL // bn, M_TOK // bm),
        in_specs=[cspec, lspec, xspec, cspec, lspec, wspec, wspec, wspec, gspec(0), gspec(1), gspec(2)],
        out_specs=pl.BlockSpec((bm, bn), lambda j, i: (i, j)),
        out_shape=jax.ShapeDtypeStruct((M_TOK, D_MODEL), BF16),
        scratch_shapes=[pltpu.VMEM((BRANCH_W, bn), BF16)] * 3,
        compiler_params=_params(("arbitrary", "arbitrary"), 56),
        name="merge_proj",
    )(hm_c, hm_l, hs, ha_c, ha_l, wm, ws, wa, big_b, big_b, big_b)


def _ffn_up_kernel(x_ref, wg_ref, wu_ref, cwg_ref, cwu_ref, cbg_ref, cbu_ref, o_ref, wgb_ref, wub_ref, *, bm):
    j = pl.program_id(0)
    i = pl.program_id(1)

    @pl.when(jnp.logical_and(i == 0, j < FFN_REAL_TILES))
    def _():
        wgb_ref[...] = wg_ref[0].astype(BF16)
        wub_ref[...] = wu_ref[0].astype(BF16)

    @pl.when(j < FFN_REAL_TILES)
    def _():
        tlen = jnp.where(i < M_CTX // bm, SEQ, DEC_SEQ)
        tpos = lax.broadcasted_iota(jnp.int32, (bm, 1), 0) & (tlen - 1)
        first = tpos == 0
        last = tpos == tlen - 1
        x = x_ref[...]

        def conv(a, cw_ref, cb_ref):
            prev = jnp.where(first, 0.0, pltpu.roll(a, 1, 0))
            nxt = jnp.where(last, 0.0, pltpu.roll(a, bm - 1, 0))
            return cw_ref[0, 0:1, :] * prev + cw_ref[0, 1:2, :] * a + cw_ref[0, 2:3, :] * nxt + cb_ref[0]

        g = conv(_dot(x, wgb_ref[...]), cwg_ref, cbg_ref)
        u = conv(_dot(x, wub_ref[...]), cwu_ref, cbu_ref)
        o_ref[...] = (g * jax.nn.sigmoid(g) * u).astype(o_ref.dtype)

    @pl.when(j >= FFN_REAL_TILES)
    def _():
        o_ref[...] = jnp.zeros_like(o_ref)


def _ffn_up(h2, ffn_up_w, conv_w, conv_b, layer):
    bm, bn = ROW_TILE, FFN_TILE
    nr = FFN_REAL_TILES

    def gcol(j):
        return jnp.minimum(j, nr - 1)

    return pl.pallas_call(
        functools.partial(_ffn_up_kernel, bm=bm),
        grid=(FFN_PAD // bn, M_TOK // bm),
        in_specs=[pl.BlockSpec((bm, D_MODEL), lambda j, i: (i, 0)),
                  pl.BlockSpec((1, D_MODEL, bn), lambda j, i: (layer, 0, gcol(j))),
                  pl.BlockSpec((1, D_MODEL, bn), lambda j, i: (layer, 0, nr + gcol(j))),
                  pl.BlockSpec((1, 3, bn), lambda j, i: (layer, 0, gcol(j))),
                  pl.BlockSpec((1, 3, bn), lambda j, i: (layer, 0, nr + gcol(j))),
                  pl.BlockSpec((1, 1, bn), lambda j, i: (layer, 0, gcol(j))),
                  pl.BlockSpec((1, 1, bn), lambda j, i: (layer, 0, nr + gcol(j)))],
        out_specs=pl.BlockSpec((bm, bn), lambda j, i: (i, j)),
        out_shape=jax.ShapeDtypeStruct((M_TOK, FFN_PAD), BF16),
        scratch_shapes=[pltpu.VMEM((D_MODEL, bn), BF16)] * 2,
        compiler_params=_params(("arbitrary", "arbitrary"), 48),
        name="ffn_up",
    )(h2, ffn_up_w, ffn_up_w, conv_w, conv_w, conv_b, conv_b)


def _cast_pad_kernel(w_ref, o_ref):
    j = pl.program_id(1)

    @pl.when(j < FFN_REAL_TILES)
    def _():
        o_ref[0] = w_ref[0].astype(BF16)

    @pl.when(j >= FFN_REAL_TILES)
    def _():
        o_ref[...] = jnp.zeros_like(o_ref)


def _cast_pad_down(ffn_down_w):
    br = FFN_TILE
    return pl.pallas_call(
        _cast_pad_kernel,
        grid=(DEPTH, FFN_PAD // br),
        in_specs=[pl.BlockSpec((1, br, D_MODEL), lambda l, j: (l, jnp.minimum(j, FFN_REAL_TILES - 1), 0))],
        out_specs=pl.BlockSpec((1, br, D_MODEL), lambda l, j: (l, j, 0)),
        out_shape=jax.ShapeDtypeStruct((DEPTH, FFN_PAD, D_MODEL), BF16),
        compiler_params=_params(("arbitrary", "arbitrary"), 32),
        name="cast_down_w",
    )(ffn_down_w)


def _ffn_down_kernel(x_ref, w_ref, r_ref, g_ref, o_ref, acc_ref, *, nk):
    k = pl.program_id(2)

    @pl.when(k == 0)
    def _():
        acc_ref[...] = jnp.zeros_like(acc_ref)

    acc_ref[...] += _dot(x_ref[...], w_ref[0])

    @pl.when(k == nk - 1)
    def _():
        o_ref[...] = r_ref[...] + g_ref[0] * acc_ref[...]


def _ffn_down(u, w, layer, res, gate):
    bm, bn, bk = ROW_TILE, 1024, FFN_PAD // 4
    nk = FFN_PAD // bk
    return pl.pallas_call(
        functools.partial(_ffn_down_kernel, nk=nk),
        grid=(D_MODEL // bn, M_TOK // bm, nk),
        in_specs=[pl.BlockSpec((bm, bk), lambda j, i, k: (i, k)),
                  pl.BlockSpec((1, bk, bn), lambda j, i, k: (layer, k, j)),
                  pl.BlockSpec((bm, bn), lambda j, i, k: (i, j)),
                  pl.BlockSpec((1, 1, bn), lambda j, i, k: (_mod_row(i, bm), 0, j))],
        out_specs=pl.BlockSpec((bm, bn), lambda j, i, k: (i, j)),
        out_shape=jax.ShapeDtypeStruct((M_TOK, D_MODEL), F32),
        scratch_shapes=[pltpu.VMEM((bm, bn), F32)],
        compiler_params=_params(("arbitrary", "arbitrary", "arbitrary"), 52),
        name="ffn_down",
    )(u, w, res, gate)


def _lane_cumsum(x, reverse):
    lane = lax.broadcasted_iota(jnp.int32, x.shape, 1)
    d = 1
    while d < LANES:
        if reverse:
            x = x + jnp.where(lane < LANES - d, pltpu.roll(x, LANES - d, 1), 0.0)
        else:
            x = x + jnp.where(lane >= d, pltpu.roll(x, d, 1), 0.0)
        d *= 2
    return x


def _mlstm_kernel(*refs, seq, hp, with_init, emit_state):
    refs = list(refs)
    q_ref, k_ref, v_ref, mo_ref, gpc_ref, gpt_ref, gain_ref = refs[:7]
    pos = 7
    if with_init:
        c0_ref, n0_ref, m0_ref = refs[pos:pos + 3]
        pos += 3
    hm_ref = refs[pos]
    pos += 1
    if emit_state:
        cf_ref, nf_ref, mf_ref = refs[pos:pos + 3]
        pos += 3
    hf_s, hb_s = refs[pos:pos + 2]

    nc = seq // CHUNK
    scale = DK_M ** -0.5

    def head_gates(head):
        def gate_rows(j):
            sub = lax.broadcasted_iota(jnp.int32, (H_M, CHUNK), 0)
            parts = [jnp.sum(jnp.where(sub == head, gpt_ref[j * H_M:(j + 1) * H_M, c * CHUNK:(c + 1) * CHUNK], 0.0),
                             axis=0, keepdims=True) for c in range(nc)]
            if nc < 8:
                parts.append(jnp.zeros((8 - nc, CHUNK), F32))
            return jnp.concatenate(parts, axis=0)

        i_f = gate_rows(0)
        b_f = _lane_cumsum(_log_sigmoid(gate_rows(1)), reverse=False)
        i_b = gate_rows(2)
        b_b = _lane_cumsum(_log_sigmoid(gate_rows(3)), reverse=True)
        rows = jnp.concatenate([b_f, i_f, b_b, i_b, jnp.zeros((LANES - 32, CHUNK), F32)], axis=0)
        return rows, rows.T

    t_idx = lax.broadcasted_iota(jnp.int32, (CHUNK, CHUNK), 0)
    s_idx = lax.broadcasted_iota(jnp.int32, (CHUNK, CHUNK), 1)
    causal = s_idx <= t_idx
    anti = s_idx >= t_idx

    def chunk_step(hh, gates, c, backward, state, h_scr):
        rows, cols = gates
        cmat, nvec, mval = state
        r0 = c * CHUNK
        base = 16 if backward else 0
        b_row = rows[base + c:base + c + 1, :]
        i_row = rows[base + 8 + c:base + 9 + c, :]
        b_col = cols[:, base + c:base + c + 1]
        i_col = cols[:, base + 8 + c:base + 9 + c]
        end = 0 if backward else CHUNK - 1

        qb = q_ref[r0:r0 + CHUNK, hh * DK_M:(hh + 1) * DK_M]
        kb = k_ref[r0:r0 + CHUNK, hh * DK_M:(hh + 1) * DK_M]
        vb = v_ref[r0:r0 + CHUNK, hh * DV_M:(hh + 1) * DV_M]

        dmat = jnp.where(anti if backward else causal, b_col + (i_row - b_row), -jnp.inf)
        g = b_col + mval
        mt = jnp.maximum(g, jnp.max(dmat, axis=1, keepdims=True))
        w_inter = jnp.exp(g - mt)
        a = jnp.exp(dmat - mt) * (_dot_nt(qb, kb) * scale)
        num = w_inter * (_dot(qb, cmat.astype(BF16)) * scale) + _dot(a.astype(BF16), vb)
        qn = jnp.sum(qb.astype(F32) * nvec, axis=1, keepdims=True) * scale
        den = w_inter * qn + jnp.sum(a, axis=1, keepdims=True)
        h_scr[r0:r0 + CHUNK, hh * DV_M:(hh + 1) * DV_M] = num / jnp.maximum(jnp.abs(den), jnp.exp(-mt))

        m_new = mt[end:end + 1, :]
        w_s = jnp.exp(b_col[end:end + 1, :] - b_col + i_col - m_new)
        decay = jnp.exp(g[end:end + 1, :] - m_new)
        kw = kb.astype(F32) * w_s
        c_new = decay * cmat + _dot(kw.T.astype(BF16), vb)
        n_new = decay * nvec + jnp.sum(kw, axis=0, keepdims=True)
        return c_new, n_new, m_new

    def init(hh, d):
        if with_init:
            return c0_ref[0, 0, d, hh], n0_ref[0, 0, d, hh], m0_ref[0, 0, d, hh][:, 0:1]
        return jnp.zeros((DK_M, DV_M), F32), jnp.zeros((1, DK_M), F32), jnp.zeros((1, 1), F32)

    gates = [head_gates(pl.program_id(1) * hp + hh) for hh in range(hp)]
    states = [[init(hh, 0), init(hh, 1)] for hh in range(hp)]
    for c in range(nc):
        for hh in range(hp):
            states[hh][0] = chunk_step(hh, gates[hh], c, False, states[hh][0], hf_s)
            states[hh][1] = chunk_step(hh, gates[hh], nc - 1 - c, True, states[hh][1], hb_s)

    if emit_state:
        for hh in range(hp):
            for d in range(2):
                cf_ref[0, d, hh] = states[hh][d][0]
                nf_ref[0, d, hh] = states[hh][d][1]
                mf_ref[0, d, hh] = jnp.broadcast_to(states[hh][d][2], (1, LANES))

    for hh in range(hp):
        hs = slice(hh * DV_M, (hh + 1) * DV_M)
        gain = gain_ref[hh]
        for c in range(nc):
            sl = slice(c * CHUNK, (c + 1) * CHUNK)
            hsum = hf_s[sl, hs] + hb_s[sl, hs]
            ms = jnp.mean(hsum * hsum, axis=-1, keepdims=True)
            y = hsum * lax.rsqrt(ms + EPS) * gain
            hm_ref[sl, hs] = (y * jax.nn.sigmoid(mo_ref[sl, hs].astype(F32))).astype(hm_ref.dtype)


def _mlstm(big_a, gpc, gpt, gain, layer, *, latent, state=None):
    seq = DEC_SEQ if latent else SEQ
    nb = DEC_BATCH if latent else BATCH
    hp = 2 if latent else 4
    r0 = (M_CTX // seq) if latent else 0
    with_init = latent
    emit_state = not latent
    qk_w, v_w = hp * DK_M, hp * DV_M

    in_specs = [
        pl.BlockSpec((seq, qk_w), lambda b, h: (r0 + b, OFF_MQ // qk_w + h)),
        pl.BlockSpec((seq, qk_w), lambda b, h: (r0 + b, OFF_MK // qk_w + h)),
        pl.BlockSpec((seq, v_w), lambda b, h: (r0 + b, OFF_MV // v_w + h)),
        pl.BlockSpec((seq, v_w), lambda b, h: (r0 + b, OFF_MO // v_w + h)),
        pl.BlockSpec((seq, LANES), lambda b, h: (r0 + b, 0)),
        pl.BlockSpec((LANES, seq), lambda b, h: (0, r0 + b)),
        pl.BlockSpec((hp, 1, DV_M), lambda b, h: (h, 0, 0)),
    ]
    args = [big_a, big_a, big_a, big_a, gpc, gpt, gain.reshape(H_M, 1, DV_M)]
    if with_init:
        st_c, st_n, st_m = state
        in_specs += [
            pl.BlockSpec((1, 1, 2, hp, DK_M, DV_M), lambda b, h: (b, layer, 0, h, 0, 0)),
            pl.BlockSpec((1, 1, 2, hp, 1, DK_M), lambda b, h: (b, layer, 0, h, 0, 0)),
            pl.BlockSpec((1, 1, 2, hp, 1, LANES), lambda b, h: (b, layer, 0, h, 0, 0)),
        ]
        args += [st_c, st_n, st_m]
    out_specs = [pl.BlockSpec((seq, v_w), lambda b, h: (b, h))]
    out_shape = [jax.ShapeDtypeStruct((nb * seq, BRANCH_W), BF16)]
    if emit_state:
        out_specs += [
            pl.BlockSpec((1, 2, hp, DK_M, DV_M), lambda b, h: (b, 0, h, 0, 0)),
            pl.BlockSpec((1, 2, hp, 1, DK_M), lambda b, h: (b, 0, h, 0, 0)),
            pl.BlockSpec((1, 2, hp, 1, LANES), lambda b, h: (b, 0, h, 0, 0)),
        ]
        out_shape += [
            jax.ShapeDtypeStruct((nb, 2, H_M, DK_M, DV_M), F32),
            jax.ShapeDtypeStruct((nb, 2, H_M, 1, DK_M), F32),
            jax.ShapeDtypeStruct((nb, 2, H_M, 1, LANES), F32),
        ]
    return pl.pallas_call(
        functools.partial(_mlstm_kernel, seq=seq, hp=hp, with_init=with_init, emit_state=emit_state),
        grid=(nb, H_M // hp),
        in_specs=in_specs,
        out_specs=out_specs,
        out_shape=out_shape,
        scratch_shapes=[pltpu.VMEM((seq, v_w), F32), pltpu.VMEM((seq, v_w), F32)],
        compiler_params=_params(("arbitrary", "arbitrary"), 32),
        name="mlstm_lat" if latent else "mlstm_ctx",
    )(*args)


def _sgu_kernel(su_ref, sv_ref, g_ref, w_ref, b_ref, o_ref):
    v = _gelu(sv_ref[...].astype(F32))
    ms = jnp.mean(v * v, axis=-1, keepdims=True)
    vn = (v * lax.rsqrt(ms + EPS) * g_ref[...]).astype(BF16)
    for g in range(SGU_GROUPS):
        sl = slice(g * SGU_GW, (g + 1) * SGU_GW)
        mixed = _dot(w_ref[g], vn[:, sl]) + b_ref[:, g:g + 1]
        o_ref[:, sl] = (_gelu(su_ref[:, sl].astype(F32)) * mixed).astype(o_ref.dtype)


def _sgu(big_b, norm_g, w_s, b_s):
    return pl.pallas_call(
        _sgu_kernel,
        grid=(M_TOK // CHUNK,),
        in_specs=[pl.BlockSpec((CHUNK, SGU_W), lambda i: (i, OFF_SU // SGU_W)),
                  pl.BlockSpec((CHUNK, SGU_W), lambda i: (i, OFF_SV // SGU_W)),
                  pl.BlockSpec((1, SGU_W), lambda i: (0, 0)),
                  pl.BlockSpec((SGU_GROUPS, CHUNK, CHUNK), lambda i: (0, 0, 0)),
                  pl.BlockSpec((CHUNK, SGU_GROUPS), lambda i: (0, 0))],
        out_specs=pl.BlockSpec((CHUNK, SGU_W), lambda i: (i, 0)),
        out_shape=jax.ShapeDtypeStruct((M_TOK, SGU_W), BF16),
        compiler_params=_params(("arbitrary",), 32),
        name="sgu",
    )(big_b, big_b, norm_g.reshape(1, SGU_W), w_s.astype(BF16), b_s.T)


def _head_sink(sink_ref, idx):
    lane = lax.broadcasted_iota(jnp.int32, (1, LANES), 1)
    return jnp.sum(jnp.where(lane == idx, sink_ref[...], 0.0), axis=1, keepdims=True)


def _ctx_attn_kernel(q_ref, k_ref, v_ref, sink_ref, o_ref, ko_ref, vo_ref):
    kv = pl.program_id(1)
    scale = HD_A ** -0.5
    kb = k_ref[...]
    vb = v_ref[...]
    ko_ref[...] = kb.astype(F32)
    vo_ref[...] = vb.astype(F32)
    for g in range(GQA_G):
        sl = slice(g * HD_A, (g + 1) * HD_A)
        sk = _head_sink(sink_ref, kv * GQA_G + g)
        s = _dot_nt(q_ref[:, sl], kb) * scale
        m = jnp.maximum(jnp.max(s, axis=1, keepdims=True), sk)
        p = jnp.exp(s - m)
        den = jnp.sum(p, axis=1, keepdims=True) + jnp.exp(sk - m)
        o_ref[:, sl] = (_dot(p.astype(BF16), vb) / den).astype(o_ref.dtype)


def _ctx_attention(big_b, sink):
    qw = GQA_G * HD_A
    kv_spec = pl.BlockSpec((SEQ, HD_A), lambda b, kv: (b, kv))
    return pl.pallas_call(
        _ctx_attn_kernel,
        grid=(BATCH, KV_A),
        in_specs=[pl.BlockSpec((SEQ, qw), lambda b, kv: (b, OFF_AQ // qw + kv)),
                  pl.BlockSpec((SEQ, HD_A), lambda b, kv: (b, OFF_AK // HD_A + kv)),
                  pl.BlockSpec((SEQ, HD_A), lambda b, kv: (b, OFF_AV // HD_A + kv)),
                  pl.BlockSpec((1, LANES), lambda b, kv: (0, 0))],
        out_specs=[pl.BlockSpec((SEQ, qw), lambda b, kv: (b, kv)), kv_spec, kv_spec],
        out_shape=[jax.ShapeDtypeStruct((M_CTX, BRANCH_W), BF16),
                   jax.ShapeDtypeStruct((M_CTX, KV_A * HD_A), F32),
                   jax.ShapeDtypeStruct((M_CTX, KV_A * HD_A), F32)],
        compiler_params=_params(("arbitrary", "arbitrary"), 32),
        name="attn_ctx",
    )(big_b, big_b, big_b, sink)


def _lat_attn_kernel(q_ref, k_ref, v_ref, kc_ref, vc_ref, cos_ref, sin_ref, sink_ref, o_ref, kr_s):
    kv = pl.program_id(1)
    scale = HD_A ** -0.5
    lane = lax.broadcasted_iota(jnp.int32, (1, HD_A), 1)
    nf = ROPE_AXIS // 2
    first_half = (lane % ROPE_AXIS) < nf

    def rope(x, sl):
        swapped = jnp.where(first_half, pltpu.roll(x, HD_A - nf, 1), pltpu.roll(x, nf, 1))
        return x * cos_ref[sl, :] + swapped * sin_ref[sl, :]

    kr_s[...] = rope(k_ref[...].astype(F32), slice(None)).astype(BF16)
    kcb = kc_ref[0, 0].astype(BF16)
    vcb = vc_ref[0, 0].astype(BF16)
    nblk = DEC_SEQ // QBLK
    for j in range(nblk):
        lo = max(j - 1, 0) * QBLK
        hi = min(j + 2, nblk) * QBLK
        qs = slice(j * QBLK, (j + 1) * QBLK)
        kband = kr_s[lo:hi, :]
        vband = v_ref[lo:hi, :]
        qpos = j * QBLK + lax.broadcasted_iota(jnp.int32, (QBLK, hi - lo), 0)
        kpos = lo + lax.broadcasted_iota(jnp.int32, (QBLK, hi - lo), 1)
        mask = jnp.abs(kpos - qpos) <= WINDOW
        for g in range(GQA_G):
            sl = slice(g * HD_A, (g + 1) * HD_A)
            sk = _head_sink(sink_ref, kv * GQA_G + g)
            qg = rope(q_ref[qs, sl].astype(F32), qs).astype(BF16)
            s_band = jnp.where(mask, _dot_nt(qg, kband) * scale, -jnp.inf)
            s_ctx = _dot_nt(qg, kcb) * scale
            m = jnp.maximum(jnp.maximum(jnp.max(s_band, axis=1, keepdims=True),
                                        jnp.max(s_ctx, axis=1, keepdims=True)), sk)
            p_band = jnp.exp(s_band - m)
            p_ctx = jnp.exp(s_ctx - m)
            den = (jnp.sum(p_band, axis=1, keepdims=True) + jnp.sum(p_ctx, axis=1, keepdims=True)
                   + jnp.exp(sk - m))
            o = _dot(p_band.astype(BF16), vband) + _dot(p_ctx.astype(BF16), vcb)
            o_ref[qs, sl] = (o / den).astype(o_ref.dtype)


def _lat_attention(big_b, cache_k, cache_v, cos_t, sin_t, sink, layer):
    qw = GQA_G * HD_A
    r0 = M_CTX // DEC_SEQ
    return pl.pallas_call(
        _lat_attn_kernel,
        grid=(DEC_BATCH, KV_A),
        in_specs=[pl.BlockSpec((DEC_SEQ, qw), lambda b, kv: (r0 + b, OFF_AQ // qw + kv)),
                  pl.BlockSpec((DEC_SEQ, HD_A), lambda b, kv: (r0 + b, OFF_AK // HD_A + kv)),
                  pl.BlockSpec((DEC_SEQ, HD_A), lambda b, kv: (r0 + b, OFF_AV // HD_A + kv)),
                  pl.BlockSpec((1, 1, PAST_LEN, HD_A), lambda b, kv: (b, layer, 0, kv)),
                  pl.BlockSpec((1, 1, PAST_LEN, HD_A), lambda b, kv: (b, layer, 0, kv)),
                  pl.BlockSpec((DEC_SEQ, HD_A), lambda b, kv: (0, 0)),
                  pl.BlockSpec((DEC_SEQ, HD_A), lambda b, kv: (0, 0)),
                  pl.BlockSpec((1, LANES), lambda b, kv: (0, 0))],
        out_specs=pl.BlockSpec((DEC_SEQ, qw), lambda b, kv: (b, kv)),
        out_shape=jax.ShapeDtypeStruct((M_LAT, BRANCH_W), BF16),
        scratch_shapes=[pltpu.VMEM((DEC_SEQ, HD_A), BF16)],
        compiler_params=_params(("arbitrary", "arbitrary"), 32),
        name="attn_lat",
    )(big_b, big_b, big_b, cache_k, cache_v, cos_t, sin_t, sink)


def _rope_tables():
    rows = DEC_SEQ // GRID_W
    row = jnp.repeat(jnp.arange(rows, dtype=F32), GRID_W)
    col = jnp.tile(jnp.arange(GRID_W, dtype=F32), rows)
    nf = ROPE_AXIS // 2
    inv = ROPE_BASE ** (-jnp.arange(nf, dtype=F32) / nf)
    ar = row[:, None] * inv[None, :]
    ac = col[:, None] * inv[None, :]
    cr, sr, cc, sc = jnp.cos(ar), jnp.sin(ar), jnp.cos(ac), jnp.sin(ac)
    cos_t = jnp.concatenate([cr, cr, cc, cc], axis=1)
    sin_t = jnp.concatenate([-sr, sr, -sc, sc], axis=1)
    return cos_t, sin_t


def _pad_cols(a, width):
    return jnp.pad(a, ((0, 0), (0, width - a.shape[1])))


def kernel(x_prompt, x_sample, cache_k, cache_v, state_C, state_n, state_m, c, c_ctx, ada_w, ada_b, norm1_g, w_in, m_gate_b, m_norm_g, sgu_norm_g, sgu_w, sgu_b, attn_sink, w_br_m, w_br_s, w_br_a, w_out, norm2_g, ffn_up, ffn_conv_w, ffn_conv_b, ffn_down, final_g):
    cs = jnp.concatenate([c, c_ctx[None, :], jnp.zeros((MOD_ROWS - DEC_BATCH - 1, D_MODEL), F32)], axis=0)
    mod = _modulation(cs, ada_w, ada_b)
    w_down = _cast_pad_down(ffn_down)
    conv_b = ffn_conv_b.reshape(DEPTH, 1, 2 * FFN_DIM)

    cos_t, sin_t = _rope_tables()
    cache_k2 = cache_k.reshape(DEC_BATCH, DEPTH, PAST_LEN, KV_A * HD_A)
    cache_v2 = cache_v.reshape(DEC_BATCH, DEPTH, PAST_LEN, KV_A * HD_A)
    st_n = state_n.reshape(DEC_BATCH, DEPTH, 2, H_M, 1, DK_M)
    st_m = jnp.broadcast_to(state_m[..., None, None], (DEC_BATCH, DEPTH, 2, H_M, 1, LANES))

    xs = (x_prompt.reshape(M_CTX, D_MODEL), x_sample.reshape(M_LAT, D_MODEL))
    ks, vs, cfs, nfs, mfs = [], [], [], [], []
    for l in range(DEPTH):
        sh1, sc1, g1, sh2, sc2, g2 = [mod[l, :, j * D_MODEL:(j + 1) * D_MODEL].reshape(MOD_ROWS, 1, D_MODEL)
                                      for j in range(6)]
        sink = _pad_cols(attn_sink[l].reshape(1, H_A), LANES)
        b_mg = _pad_cols(m_gate_b[l].reshape(1, N_MG), LANES)

        h = _norm_mod(xs, norm1_g[l], sc1, sh1)
        big_a = _in_proj_a(h, w_in, l)
        big_b = _in_proj_b(h, w_in, l)
        gpc, gpt = _gate_proj(h, w_in, b_mg, l)

        hm_c, cf, nf, mf = _mlstm(big_a, gpc, gpt, m_norm_g[l], l, latent=False)
        (hm_l,) = _mlstm(big_a, gpc, gpt, m_norm_g[l], l, latent=True, state=(state_C, st_n, st_m))
        hs = _sgu(big_b, sgu_norm_g[l], sgu_w[l], sgu_b[l])
        ha_c, k_new, v_new = _ctx_attention(big_b, sink)
        ha_l = _lat_attention(big_b, cache_k2, cache_v2, cos_t, sin_t, sink, l)

        y = _merge_proj(hm_c, hm_l, hs, ha_c, ha_l, w_br_m, w_br_s, w_br_a, big_b, l)
        x = _out_proj(y, w_out, l, xs, g1)

        h2 = _norm_mod((x,), norm2_g[l], sc2, sh2)
        u = _ffn_up(h2, ffn_up, ffn_conv_w, conv_b, l)
        x = _ffn_down(u, w_down, l, x, g2)
        xs = (x,)

        ks.append(k_new.reshape(BATCH, SEQ, KV_A, HD_A))
        vs.append(v_new.reshape(BATCH, SEQ, KV_A, HD_A))
        cfs.append(cf)
        nfs.append(nf.reshape(BATCH, 2, H_M, DK_M))
        mfs.append(mf[:, :, :, 0, 0])

    y_prompt = _final_norm(x, final_g, 0, M_CTX).reshape(BATCH, SEQ, D_MODEL)
    y_sample = _final_norm(x, final_g, M_CTX, M_LAT).reshape(DEC_BATCH, DEC_SEQ, D_MODEL)
    return (y_prompt, y_sample, jnp.stack(ks, axis=1), jnp.stack(vs, axis=1),
            jnp.stack(cfs, axis=1), jnp.stack(nfs, axis=1), jnp.stack(mfs, axis=1))
```

```python
import functools
import math

import numpy as np
import jax
import jax.numpy as jnp
from jax import lax
from jax.experimental import pallas as pl
from jax.experimental.pallas import tpu as pltpu

D_MODEL = 4096
BATCH = 32
SEQ = 256
DEPTH = 2
DEC_BATCH = 4
DEC_SEQ = 1024
PAST_LEN = 512
GRID_W = 64
BRANCH_W = D_MODEL // 2
CHUNK = 128
H_M = 8
DV_M = BRANCH_W // H_M
DK_M = DV_M // 2
SGU_W = BRANCH_W
SGU_GROUPS = 8
SGU_GW = SGU_W // SGU_GROUPS
H_A = 16
KV_A = 4
HD_A = BRANCH_W // H_A
GQA_G = H_A // KV_A
WINDOW = 128
QBLK = 128
ROPE_BASE = 10000.0
ROPE_AXIS = HD_A // 2
FFN_DIM = 11008
EPS = 1e-6

F32 = jnp.float32
BF16 = jnp.bfloat16

M_CTX = BATCH * SEQ
M_LAT = DEC_BATCH * DEC_SEQ
M_TOK = M_CTX + M_LAT
MOD_ROWS = 8
CTX_MOD_ROW = DEC_BATCH

LANES = 128
MIB = 1024 * 1024

N_MG = 4 * H_M
OFF_MQ = 0
OFF_MK = OFF_MQ + H_M * DK_M
OFF_MV = OFF_MK + H_M * DK_M
OFF_MO = OFF_MV + BRANCH_W
N_A = OFF_MO + BRANCH_W
OFF_SU = N_A
OFF_SV = OFF_SU + SGU_W
OFF_AQ = OFF_SV + SGU_W
OFF_AK = OFF_AQ + H_A * HD_A
OFF_AV = OFF_AK + KV_A * HD_A
OFF_GATES = OFF_AV + KV_A * HD_A
N_MAIN = OFF_GATES + 3 * D_MODEL

FFN_TILE = 256
FFN_PAD = 11264
FFN_REAL_TILES = FFN_DIM // FFN_TILE
ROW_TILE = 1024
W_TILE = 512

_GELU_C = float(np.float32(math.sqrt(2.0 / math.pi)))


def _params(sem, vmem_mib):
    return pltpu.CompilerParams(dimension_semantics=sem, vmem_limit_bytes=vmem_mib * MIB)


def _mod_row(i, bm):
    n_ctx = M_CTX // bm
    per_batch = DEC_SEQ // bm
    return jnp.where(i < n_ctx, CTX_MOD_ROW, (i - n_ctx) // per_batch)


def _ctx_tile(i, bm):
    return jnp.minimum(i, M_CTX // bm - 1)


def _lat_tile(i, bm):
    return jnp.maximum(i - M_CTX // bm, 0)


def _dot(a, b):
    return jnp.dot(a, b, preferred_element_type=F32)


def _dot_nt(a, b):
    return lax.dot_general(a, b, (((1,), (1,)), ((), ())), preferred_element_type=F32)


def _gelu(x):
    return 0.5 * x * (1.0 + jnp.tanh(_GELU_C * (x + 0.044715 * (x * x * x))))


def _log_sigmoid(x):
    return jnp.minimum(x, 0.0) - jnp.log(1.0 + jnp.exp(-jnp.abs(x)))


def _mod_kernel(c_ref, w_ref, b_ref, o_ref):
    c = c_ref[...]
    s = (c * jax.nn.sigmoid(c)).astype(BF16)
    o_ref[0] = _dot(s, w_ref[0].astype(BF16)) + b_ref[0]


def _modulation(cs, ada_w, ada_b):
    bn = 512
    n = ada_w.shape[-1]
    return pl.pallas_call(
        _mod_kernel,
        grid=(DEPTH, n // bn),
        in_specs=[pl.BlockSpec((MOD_ROWS, D_MODEL), lambda l, j: (0, 0)),
                  pl.BlockSpec((1, D_MODEL, bn), lambda l, j: (l, 0, j)),
                  pl.BlockSpec((1, 1, bn), lambda l, j: (l, 0, j))],
        out_specs=pl.BlockSpec((1, MOD_ROWS, bn), lambda l, j: (l, 0, j)),
        out_shape=jax.ShapeDtypeStruct((DEPTH, MOD_ROWS, n), F32),
        compiler_params=_params(("arbitrary", "arbitrary"), 40),
        name="adaln_mod",
    )(cs, ada_w, ada_b.reshape(DEPTH, 1, n))


def _norm_mod_body(x, g_ref, sc_ref, sh_ref, o_ref):
    ms = jnp.mean(x * x, axis=-1, keepdims=True)
    y = x * lax.rsqrt(ms + EPS) * g_ref[...]
    o_ref[...] = (y * (1.0 + sc_ref[0]) + sh_ref[0]).astype(o_ref.dtype)


def _norm_mod_kernel(x_ref, g_ref, sc_ref, sh_ref, o_ref):
    _norm_mod_body(x_ref[...], g_ref, sc_ref, sh_ref, o_ref)


def _norm_mod_split_kernel(xc_ref, xl_ref, g_ref, sc_ref, sh_ref, o_ref, *, bm):
    i = pl.program_id(0)

    @pl.when(i < M_CTX // bm)
    def _():
        _norm_mod_body(xc_ref[...], g_ref, sc_ref, sh_ref, o_ref)

    @pl.when(i >= M_CTX // bm)
    def _():
        _norm_mod_body(xl_ref[...], g_ref, sc_ref, sh_ref, o_ref)


def _norm_mod(xs, g, sc, sh):
    bm = 256
    mod_spec = pl.BlockSpec((1, 1, D_MODEL), lambda i: (_mod_row(i, bm), 0, 0))
    if len(xs) == 1:
        kern = _norm_mod_kernel
        x_specs = [pl.BlockSpec((bm, D_MODEL), lambda i: (i, 0))]
    else:
        kern = functools.partial(_norm_mod_split_kernel, bm=bm)
        x_specs = [pl.BlockSpec((bm, D_MODEL), lambda i: (_ctx_tile(i, bm), 0)),
                   pl.BlockSpec((bm, D_MODEL), lambda i: (_lat_tile(i, bm), 0))]
    return pl.pallas_call(
        kern,
        grid=(M_TOK // bm,),
        in_specs=x_specs + [pl.BlockSpec((1, D_MODEL), lambda i: (0, 0)), mod_spec, mod_spec],
        out_specs=pl.BlockSpec((bm, D_MODEL), lambda i: (i, 0)),
        out_shape=jax.ShapeDtypeStruct((M_TOK, D_MODEL), BF16),
        compiler_params=_params(("arbitrary",), 32),
        name="norm_mod",
    )(*xs, g.reshape(1, D_MODEL), sc, sh)


def _final_norm_kernel(x_ref, g_ref, o_ref):
    x = x_ref[...]
    ms = jnp.mean(x * x, axis=-1, keepdims=True)
    o_ref[...] = x * lax.rsqrt(ms + EPS) * g_ref[...]


def _final_norm(x, g, row0, rows):
    bm = 256
    b0 = row0 // bm
    return pl.pallas_call(
        _final_norm_kernel,
        grid=(rows // bm,),
        in_specs=[pl.BlockSpec((bm, D_MODEL), lambda i: (b0 + i, 0)),
                  pl.BlockSpec((1, D_MODEL), lambda i: (0, 0))],
        out_specs=pl.BlockSpec((bm, D_MODEL), lambda i: (i, 0)),
        out_shape=jax.ShapeDtypeStruct((rows, D_MODEL), F32),
        compiler_params=_params(("arbitrary",), 32),
        name="final_norm",
    )(x, g.reshape(1, D_MODEL))


W_CHUNKS = 8


def _in_proj_kernel(x_ref, wt_ref, o_ref, wb_ref, buf_ref, sem, *, layer, bn, n_tiles):
    j = pl.program_id(0)
    i = pl.program_id(1)
    ch = bn // W_CHUNKS

    def chunk_copy(tile, c):
        row0 = tile * bn + jnp.where(tile >= N_A // bn, N_MG, 0) + c * ch
        return pltpu.make_async_copy(wt_ref.at[layer, pl.ds(pl.multiple_of(row0, 8), ch), :], buf_ref, sem)

    def round_chunk(slot, c):
        wb_ref[slot, pl.ds(pl.multiple_of(c * ch, ch), ch), :] = buf_ref[...].astype(BF16)

    @pl.when(jnp.logical_and(j == 0, i == 0))
    def _():
        def body(c, carry):
            cp = chunk_copy(0, c)
            cp.start()
            cp.wait()
            round_chunk(0, c)
            return carry

        lax.fori_loop(0, W_CHUNKS, body, 0)

    prefetch = jnp.logical_and(i < W_CHUNKS, j + 1 < n_tiles)

    @pl.when(prefetch)
    def _():
        chunk_copy(j + 1, i).start()

    o_ref[...] = _dot_nt(x_ref[...], wb_ref[j % 2]).astype(o_ref.dtype)

    @pl.when(prefetch)
    def _():
        chunk_copy(j + 1, i).wait()
        round_chunk((j + 1) % 2, i)


def _in_proj(h, w_in_t, layer):
    bm, bn = ROW_TILE, 1024
    n_tiles = N_MAIN // bn
    return pl.pallas_call(
        functools.partial(_in_proj_kernel, layer=layer, bn=bn, n_tiles=n_tiles),
        grid=(n_tiles, M_TOK // bm),
        in_specs=[pl.BlockSpec((bm, D_MODEL), lambda j, i: (i, 0)),
                  pl.BlockSpec(memory_space=pl.ANY)],
        out_specs=pl.BlockSpec((bm, bn), lambda j, i: (i, j)),
        out_shape=jax.ShapeDtypeStruct((M_TOK, N_MAIN), BF16),
        scratch_shapes=[pltpu.VMEM((2, bn, D_MODEL), BF16),
                        pltpu.VMEM((bn // W_CHUNKS, D_MODEL), F32),
                        pltpu.SemaphoreType.DMA(())],
        compiler_params=_params(("arbitrary", "arbitrary"), 48),
        name="in_proj",
    )(h, w_in_t)


def _gate_kernel(x_ref, w_ref, b_ref, o_ref):
    o_ref[...] = _dot_nt(w_ref[0].astype(BF16), x_ref[...]) + b_ref[...]


def _gate_proj(h, w_in_t, b_mg, layer):
    bm = ROW_TILE
    return pl.pallas_call(
        _gate_kernel,
        grid=(M_TOK // bm,),
        in_specs=[pl.BlockSpec((bm, D_MODEL), lambda i: (i, 0)),
                  pl.BlockSpec((1, N_MG, D_MODEL), lambda i: (layer, N_A // N_MG, 0)),
                  pl.BlockSpec((N_MG, 1), lambda i: (0, 0))],
        out_specs=pl.BlockSpec((N_MG, bm), lambda i: (0, i)),
        out_shape=jax.ShapeDtypeStruct((N_MG, M_TOK), F32),
        compiler_params=_params(("arbitrary",), 32),
        name="gate_proj",
    )(h, w_in_t, b_mg)


def _out_proj_kernel(*refs, bm, split):
    if split:
        x_ref, w_ref, rc_ref, rl_ref, g_ref, o_ref, wb_ref = refs
    else:
        x_ref, w_ref, r_ref, g_ref, o_ref, wb_ref = refs
    i = pl.program_id(1)

    @pl.when(i == 0)
    def _():
        wb_ref[...] = w_ref[0].astype(BF16)

    y = g_ref[0] * _dot(x_ref[...], wb_ref[...])
    if split:
        @pl.when(i < M_CTX // bm)
        def _():
            o_ref[...] = rc_ref[...] + y

        @pl.when(i >= M_CTX // bm)
        def _():
            o_ref[...] = rl_ref[...] + y
    else:
        o_ref[...] = r_ref[...] + y


def _out_proj(y, w_out, layer, res, gate):
    bm, bn = ROW_TILE, W_TILE
    split = len(res) == 2
    if split:
        r_specs = [pl.BlockSpec((bm, bn), lambda j, i: (_ctx_tile(i, bm), j)),
                   pl.BlockSpec((bm, bn), lambda j, i: (_lat_tile(i, bm), j))]
    else:
        r_specs = [pl.BlockSpec((bm, bn), lambda j, i: (i, j))]
    return pl.pallas_call(
        functools.partial(_out_proj_kernel, bm=bm, split=split),
        grid=(D_MODEL // bn, M_TOK // bm),
        in_specs=[pl.BlockSpec((bm, D_MODEL), lambda j, i: (i, 0)),
                  pl.BlockSpec((1, D_MODEL, bn), lambda j, i: (layer, 0, j))] + r_specs + [
                  pl.BlockSpec((1, 1, bn), lambda j, i: (_mod_row(i, bm), 0, j))],
        out_specs=pl.BlockSpec((bm, bn), lambda j, i: (i, j)),
        out_shape=jax.ShapeDtypeStruct((M_TOK, D_MODEL), F32),
        scratch_shapes=[pltpu.VMEM((D_MODEL, bn), BF16)],
        compiler_params=_params(("arbitrary", "arbitrary"), 52),
        name="out_proj",
    )(y, w_out, *res, gate)


def _merge_kernel(hmc_ref, hml_ref, hs_ref, hac_ref, hal_ref, wm_ref, ws_ref, wa_ref,
                  gm_ref, gs_ref, ga_ref, o_ref, wmb_ref, wsb_ref, wab_ref, *, bm):
    i = pl.program_id(1)

    @pl.when(i == 0)
    def _():
        wmb_ref[...] = wm_ref[0].astype(BF16)
        wsb_ref[...] = ws_ref[0].astype(BF16)
        wab_ref[...] = wa_ref[0].astype(BF16)

    def gate(ref):
        return jax.nn.sigmoid(ref[...].astype(F32))

    def finish(hm, ha):
        y = gate(gm_ref) * _dot(hm, wmb_ref[...])
        y = y + gate(gs_ref) * _dot(hs_ref[...], wsb_ref[...])
        y = y + gate(ga_ref) * _dot(ha, wab_ref[...])
        o_ref[...] = y.astype(o_ref.dtype)

    @pl.when(i < M_CTX // bm)
    def _():
        finish(hmc_ref[...], hac_ref[...])

    @pl.when(i >= M_CTX // bm)
    def _():
        finish(hml_ref[...], hal_ref[...])


def _merge_proj(hm_c, hm_l, hs, ha_c, ha_l, wm, ws, wa, big, layer):
    bm, bn = 256, 1024
    gate0 = OFF_GATES // bn
    per = D_MODEL // bn
    cspec = pl.BlockSpec((bm, BRANCH_W), lambda j, i: (_ctx_tile(i, bm), 0))
    lspec = pl.BlockSpec((bm, BRANCH_W), lambda j, i: (_lat_tile(i, bm), 0))
    xspec = pl.BlockSpec((bm, BRANCH_W), lambda j, i: (i, 0))
    wspec = pl.BlockSpec((1, BRANCH_W, bn), lambda j, i: (layer, 0, j), pipeline_mode=pl.Buffered(1))

    def gspec(b):
        return pl.BlockSpec((bm, bn), lambda j, i: (i, gate0 + b * per + j))

    return pl.pallas_call(
        functools.partial(_merge_kernel, bm=bm),
        grid=(D_MODEL // bn, M_TOK // bm),
        in_specs=[cspec, lspec, xspec, cspec, lspec, wspec, wspec, wspec, gspec(0), gspec(1), gspec(2)],
        out_specs=pl.BlockSpec((bm, bn), lambda j, i: (i, j)),
        out_shape=jax.ShapeDtypeStruct((M_TOK, D_MODEL), BF16),
        scratch_shapes=[pltpu.VMEM((BRANCH_W, bn), BF16)] * 3,
        compiler_params=_params(("arbitrary", "arbitrary"), 56),
        name="merge_proj",
    )(hm_c, hm_l, hs, ha_c, ha_l, wm, ws, wa, big, big, big)


def _ffn_up_kernel(x_ref, w_ref, cwg_ref, cwu_ref, cbg_ref, cbu_ref, o_ref,
                   wg_ref, wu_ref, bufg_ref, bufu_ref, sems, *, layer, bm, bn, n_tiles):
    j = pl.program_id(0)
    i = pl.program_id(1)
    ch = D_MODEL // W_CHUNKS
    tail = FFN_DIM - (n_tiles - 1) * bn

    def rows(c):
        return pl.ds(pl.multiple_of(c * ch, ch), ch)

    def gate_copy(tile, c):
        cols = pl.ds(pl.multiple_of(tile * bn, LANES), bn)
        return pltpu.make_async_copy(w_ref.at[layer, rows(c), cols], bufg_ref, sems.at[0])

    def up_copy(tile, c):
        cols = pl.ds(pl.multiple_of(FFN_DIM + tile * bn, LANES), bn)
        return pltpu.make_async_copy(w_ref.at[layer, rows(c), cols], bufu_ref, sems.at[1])

    def up_tail_copy(c):
        cols = pl.ds(FFN_DIM + (n_tiles - 1) * bn, tail)
        return pltpu.make_async_copy(w_ref.at[layer, rows(c), cols], bufu_ref.at[:, pl.ds(0, tail)], sems.at[1])

    def round_chunks(slot, c, is_tail):
        wg_ref[slot, rows(c), :] = bufg_ref[...].astype(BF16)
        if is_tail:
            wu_ref[slot, rows(c), 0:tail] = bufu_ref[:, 0:tail].astype(BF16)
            wu_ref[slot, rows(c), tail:bn] = jnp.zeros((ch, bn - tail), BF16)
        else:
            wu_ref[slot, rows(c), :] = bufu_ref[...].astype(BF16)

    @pl.when(jnp.logical_and(j == 0, i == 0))
    def _():
        def body(c, carry):
            cg, cu = gate_copy(0, c), up_copy(0, c)
            cg.start()
            cu.start()
            cg.wait()
            cu.wait()
            round_chunks(0, c, False)
            return carry

        lax.fori_loop(0, W_CHUNKS, body, 0)

    prefetch = jnp.logical_and(i < W_CHUNKS, j + 1 < n_tiles)
    to_tail = j + 1 == n_tiles - 1
    pre_full = jnp.logical_and(prefetch, jnp.logical_not(to_tail))
    pre_tail = jnp.logical_and(prefetch, to_tail)

    @pl.when(pre_full)
    def _():
        gate_copy(j + 1, i).start()
        up_copy(j + 1, i).start()

    @pl.when(pre_tail)
    def _():
        gate_copy(j + 1, i).start()
        up_tail_copy(i).start()

    tlen = jnp.where(i < M_CTX // bm, SEQ, DEC_SEQ)
    tpos = lax.broadcasted_iota(jnp.int32, (bm, 1), 0) & (tlen - 1)
    first = tpos == 0
    last = tpos == tlen - 1
    x = x_ref[...]

    def conv(a, cw_ref, cb_ref):
        prev = jnp.where(first, 0.0, pltpu.roll(a, 1, 0))
        nxt = jnp.where(last, 0.0, pltpu.roll(a, bm - 1, 0))
        return cw_ref[0, 0:1, :] * prev + cw_ref[0, 1:2, :] * a + cw_ref[0, 2:3, :] * nxt + cb_ref[0]

    g = conv(_dot(x, wg_ref[j % 2]), cwg_ref, cbg_ref)
    u = conv(_dot(x, wu_ref[j % 2]), cwu_ref, cbu_ref)
    o_ref[...] = (g * jax.nn.sigmoid(g) * u).astype(o_ref.dtype)

    @pl.when(pre_full)
    def _():
        gate_copy(j + 1, i).wait()
        up_copy(j + 1, i).wait()
        round_chunks((j + 1) % 2, i, False)

    @pl.when(pre_tail)
    def _():
        gate_copy(j + 1, i).wait()
        up_tail_copy(i).wait()
        round_chunks((j + 1) % 2, i, True)


def _ffn_up(h2, ffn_up_w, cwg, cwu, cbg, cbu, layer):
    bm, bn = ROW_TILE, 512
    n_tiles = FFN_PAD // bn
    cw_spec = pl.BlockSpec((1, 3, bn), lambda j, i: (layer, 0, j))
    cb_spec = pl.BlockSpec((1, 1, bn), lambda j, i: (layer, 0, j))
    return pl.pallas_call(
        functools.partial(_ffn_up_kernel, layer=layer, bm=bm, bn=bn, n_tiles=n_tiles),
        grid=(n_tiles, M_TOK // bm),
        in_specs=[pl.BlockSpec((bm, D_MODEL), lambda j, i: (i, 0)),
                  pl.BlockSpec(memory_space=pl.ANY),
                  cw_spec, cw_spec, cb_spec, cb_spec],
        out_specs=pl.BlockSpec((bm, bn), lambda j, i: (i, j)),
        out_shape=jax.ShapeDtypeStruct((M_TOK, FFN_PAD), BF16),
        scratch_shapes=[pltpu.VMEM((2, D_MODEL, bn), BF16), pltpu.VMEM((2, D_MODEL, bn), BF16),
                        pltpu.VMEM((D_MODEL // W_CHUNKS, bn), F32), pltpu.VMEM((D_MODEL // W_CHUNKS, bn), F32),
                        pltpu.SemaphoreType.DMA((2,))],
        compiler_params=_params(("arbitrary", "arbitrary"), 56),
        name="ffn_up",
    )(h2, ffn_up_w, cwg, cwu, cbg, cbu)


def _cast_pad_kernel(w_ref, o_ref):
    j = pl.program_id(1)

    @pl.when(j < FFN_REAL_TILES)
    def _():
        o_ref[0] = w_ref[0].astype(BF16)

    @pl.when(j >= FFN_REAL_TILES)
    def _():
        o_ref[...] = jnp.zeros_like(o_ref)


def _cast_pad_down(ffn_down_w):
    br = FFN_TILE
    return pl.pallas_call(
        _cast_pad_kernel,
        grid=(DEPTH, FFN_PAD // br),
        in_specs=[pl.BlockSpec((1, br, D_MODEL), lambda l, j: (l, jnp.minimum(j, FFN_REAL_TILES - 1), 0))],
        out_specs=pl.BlockSpec((1, br, D_MODEL), lambda l, j: (l, j, 0)),
        out_shape=jax.ShapeDtypeStruct((DEPTH, FFN_PAD, D_MODEL), BF16),
        compiler_params=_params(("arbitrary", "arbitrary"), 32),
        name="cast_down_w",
    )(ffn_down_w)


def _ffn_down_kernel(x_ref, w_ref, r_ref, g_ref, o_ref, acc_ref, *, nk):
    k = pl.program_id(2)

    @pl.when(k == 0)
    def _():
        acc_ref[...] = jnp.zeros_like(acc_ref)

    acc_ref[...] += _dot(x_ref[...], w_ref[0])

    @pl.when(k == nk - 1)
    def _():
        o_ref[...] = r_ref[...] + g_ref[0] * acc_ref[...]


def _ffn_down(u, w, layer, res, gate):
    bm, bn, bk = ROW_TILE, 1024, FFN_PAD // 4
    nk = FFN_PAD // bk
    return pl.pallas_call(
        functools.partial(_ffn_down_kernel, nk=nk),
        grid=(D_MODEL // bn, M_TOK // bm, nk),
        in_specs=[pl.BlockSpec((bm, bk), lambda j, i, k: (i, k)),
                  pl.BlockSpec((1, bk, bn), lambda j, i, k: (layer, k, j)),
                  pl.BlockSpec((bm, bn), lambda j, i, k: (i, j)),
                  pl.BlockSpec((1, 1, bn), lambda j, i, k: (_mod_row(i, bm), 0, j))],
        out_specs=pl.BlockSpec((bm, bn), lambda j, i, k: (i, j)),
        out_shape=jax.ShapeDtypeStruct((M_TOK, D_MODEL), F32),
        scratch_shapes=[pltpu.VMEM((bm, bn), F32)],
        compiler_params=_params(("arbitrary", "arbitrary", "arbitrary"), 52),
        name="ffn_down",
    )(u, w, res, gate)


def _lane_cumsum(x, reverse):
    lane = lax.broadcasted_iota(jnp.int32, x.shape, 1)
    d = 1
    while d < LANES:
        if reverse:
            x = x + jnp.where(lane < LANES - d, pltpu.roll(x, LANES - d, 1), 0.0)
        else:
            x = x + jnp.where(lane >= d, pltpu.roll(x, d, 1), 0.0)
        d *= 2
    return x


def _mlstm_kernel(*refs, seq, hp, with_init, emit_state):
    refs = list(refs)
    q_ref, k_ref, v_ref, mo_ref, gpt_ref, gain_ref = refs[:6]
    pos = 6
    if with_init:
        c0_ref, n0_ref, m0_ref = refs[pos:pos + 3]
        pos += 3
    hm_ref = refs[pos]
    pos += 1
    if emit_state:
        cf_ref, nf_ref, mf_ref = refs[pos:pos + 3]
        pos += 3
    hf_s, hb_s = refs[pos:pos + 2]

    nc = seq // CHUNK
    scale = DK_M ** -0.5

    def head_gates(head):
        def gate_rows(j):
            sub = lax.broadcasted_iota(jnp.int32, (H_M, CHUNK), 0)
            parts = [jnp.sum(jnp.where(sub == head, gpt_ref[j * H_M:(j + 1) * H_M, c * CHUNK:(c + 1) * CHUNK], 0.0),
                             axis=0, keepdims=True) for c in range(nc)]
            if nc < 8:
                parts.append(jnp.zeros((8 - nc, CHUNK), F32))
            return jnp.concatenate(parts, axis=0)

        i_f = gate_rows(0)
        b_f = _lane_cumsum(_log_sigmoid(gate_rows(1)), reverse=False)
        i_b = gate_rows(2)
        b_b = _lane_cumsum(_log_sigmoid(gate_rows(3)), reverse=True)
        rows = jnp.concatenate([b_f, i_f, b_b, i_b, jnp.zeros((LANES - 32, CHUNK), F32)], axis=0)
        return rows, rows.T

    t_idx = lax.broadcasted_iota(jnp.int32, (CHUNK, CHUNK), 0)
    s_idx = lax.broadcasted_iota(jnp.int32, (CHUNK, CHUNK), 1)
    causal = s_idx <= t_idx
    anti = s_idx >= t_idx

    def chunk_step(hh, gates, c, backward, state, h_scr):
        rows, cols = gates
        cmat, nvec, mval = state
        r0 = c * CHUNK
        base = 16 if backward else 0
        b_row = rows[base + c:base + c + 1, :]
        i_row = rows[base + 8 + c:base + 9 + c, :]
        b_col = cols[:, base + c:base + c + 1]
        i_col = cols[:, base + 8 + c:base + 9 + c]
        end = 0 if backward else CHUNK - 1

        qb = q_ref[r0:r0 + CHUNK, hh * DK_M:(hh + 1) * DK_M]
        kb = k_ref[r0:r0 + CHUNK, hh * DK_M:(hh + 1) * DK_M]
        vb = v_ref[r0:r0 + CHUNK, hh * DV_M:(hh + 1) * DV_M]

        dmat = jnp.where(anti if backward else causal, b_col + (i_row - b_row), -jnp.inf)
        g = b_col + mval
        mt = jnp.maximum(g, jnp.max(dmat, axis=1, keepdims=True))
        w_inter = jnp.exp(g - mt)
        a = jnp.exp(dmat - mt) * (_dot_nt(qb, kb) * scale)
        num = w_inter * (_dot(qb, cmat.astype(BF16)) * scale) + _dot(a.astype(BF16), vb)
        qn = jnp.sum(qb.astype(F32) * nvec, axis=1, keepdims=True) * scale
        den = w_inter * qn + jnp.sum(a, axis=1, keepdims=True)
        h_scr[r0:r0 + CHUNK, hh * DV_M:(hh + 1) * DV_M] = num / jnp.maximum(jnp.abs(den), jnp.exp(-mt))

        m_new = mt[end:end + 1, :]
        w_s = jnp.exp(b_col[end:end + 1, :] - b_col + i_col - m_new)
        decay = jnp.exp(g[end:end + 1, :] - m_new)
        kw = kb.astype(F32) * w_s
        c_new = decay * cmat + _dot(kw.T.astype(BF16), vb)
        n_new = decay * nvec + jnp.sum(kw, axis=0, keepdims=True)
        return c_new, n_new, m_new

    def init(hh, d):
        if with_init:
            return c0_ref[0, 0, d, hh], n0_ref[0, 0, d, hh], m0_ref[0, 0, d, hh][:, 0:1]
        return jnp.zeros((DK_M, DV_M), F32), jnp.zeros((1, DK_M), F32), jnp.zeros((1, 1), F32)

    gates = [head_gates(pl.program_id(1) * hp + hh) for hh in range(hp)]
    states = [[init(hh, 0), init(hh, 1)] for hh in range(hp)]
    for c in range(nc):
        for hh in range(hp):
            states[hh][0] = chunk_step(hh, gates[hh], c, False, states[hh][0], hf_s)
            states[hh][1] = chunk_step(hh, gates[hh], nc - 1 - c, True, states[hh][1], hb_s)

    if emit_state:
        for hh in range(hp):
            for d in range(2):
                cf_ref[0, d, hh] = states[hh][d][0]
                nf_ref[0, d, hh] = states[hh][d][1]
                mf_ref[0, d, hh] = jnp.broadcast_to(states[hh][d][2], (1, LANES))

    for hh in range(hp):
        hs = slice(hh * DV_M, (hh + 1) * DV_M)
        gain = gain_ref[hh]
        for c in range(nc):
            sl = slice(c * CHUNK, (c + 1) * CHUNK)
            hsum = hf_s[sl, hs] + hb_s[sl, hs]
            ms = jnp.mean(hsum * hsum, axis=-1, keepdims=True)
            y = hsum * lax.rsqrt(ms + EPS) * gain
            hm_ref[sl, hs] = (y * jax.nn.sigmoid(mo_ref[sl, hs].astype(F32))).astype(hm_ref.dtype)


def _mlstm(big, gpt, gain, layer, *, latent, state=None):
    seq = DEC_SEQ if latent else SEQ
    nb = DEC_BATCH if latent else BATCH
    hp = 1 if latent else 4
    r0 = (M_CTX // seq) if latent else 0
    with_init = latent
    emit_state = not latent
    qk_w, v_w = hp * DK_M, hp * DV_M

    in_specs = [
        pl.BlockSpec((seq, qk_w), lambda b, h: (r0 + b, OFF_MQ // qk_w + h)),
        pl.BlockSpec((seq, qk_w), lambda b, h: (r0 + b, OFF_MK // qk_w + h)),
        pl.BlockSpec((seq, v_w), lambda b, h: (r0 + b, OFF_MV // v_w + h)),
        pl.BlockSpec((seq, v_w), lambda b, h: (r0 + b, OFF_MO // v_w + h)),
        pl.BlockSpec((N_MG, seq), lambda b, h: (0, r0 + b)),
        pl.BlockSpec((hp, 1, DV_M), lambda b, h: (h, 0, 0)),
    ]
    args = [big, big, big, big, gpt, gain.reshape(H_M, 1, DV_M)]
    if with_init:
        st_c, st_n, st_m = state
        in_specs += [
            pl.BlockSpec((1, 1, 2, hp, DK_M, DV_M), lambda b, h: (b, layer, 0, h, 0, 0)),
            pl.BlockSpec((1, 1, 2, hp, 1, DK_M), lambda b, h: (b, layer, 0, h, 0, 0)),
            pl.BlockSpec((1, 1, 2, hp, 1, LANES), lambda b, h: (b, layer, 0, h, 0, 0)),
        ]
        args += [st_c, st_n, st_m]
    out_specs = [pl.BlockSpec((seq, v_w), lambda b, h: (b, h))]
    out_shape = [jax.ShapeDtypeStruct((nb * seq, BRANCH_W), BF16)]
    if emit_state:
        out_specs += [
            pl.BlockSpec((1, 2, hp, DK_M, DV_M), lambda b, h: (b, 0, h, 0, 0)),
            pl.BlockSpec((1, 2, hp, 1, DK_M), lambda b, h: (b, 0, h, 0, 0)),
            pl.BlockSpec((1, 2, hp, 1, LANES), lambda b, h: (b, 0, h, 0, 0)),
        ]
        out_shape += [
            jax.ShapeDtypeStruct((nb, 2, H_M, DK_M, DV_M), F32),
            jax.ShapeDtypeStruct((nb, 2, H_M, 1, DK_M), F32),
            jax.ShapeDtypeStruct((nb, 2, H_M, 1, LANES), F32),
        ]
    return pl.pallas_call(
        functools.partial(_mlstm_kernel, seq=seq, hp=hp, with_init=with_init, emit_state=emit_state),
        grid=(nb, H_M // hp),
        in_specs=in_specs,
        out_specs=out_specs,
        out_shape=out_shape,
        scratch_shapes=[pltpu.VMEM((seq, v_w), F32), pltpu.VMEM((seq, v_w), F32)],
        compiler_params=_params(("arbitrary", "arbitrary"), 32),
        name="mlstm_lat" if latent else "mlstm_ctx",
    )(*args)


def _sgu_kernel(su_ref, sv_ref, g_ref, w_ref, b_ref, o_ref):
    v = _gelu(sv_ref[...].astype(F32))
    ms = jnp.mean(v * v, axis=-1, keepdims=True)
    vn = (v * lax.rsqrt(ms + EPS) * g_ref[...]).astype(BF16)
    for g in range(SGU_GROUPS):
        sl = slice(g * SGU_GW, (g + 1) * SGU_GW)
        mixed = _dot(w_ref[g], vn[:, sl]) + b_ref[:, g:g + 1]
        o_ref[:, sl] = (_gelu(su_ref[:, sl].astype(F32)) * mixed).astype(o_ref.dtype)


def _sgu(big, norm_g, w_s, b_s):
    return pl.pallas_call(
        _sgu_kernel,
        grid=(M_TOK // CHUNK,),
        in_specs=[pl.BlockSpec((CHUNK, SGU_W), lambda i: (i, OFF_SU // SGU_W)),
                  pl.BlockSpec((CHUNK, SGU_W), lambda i: (i, OFF_SV // SGU_W)),
                  pl.BlockSpec((1, SGU_W), lambda i: (0, 0)),
                  pl.BlockSpec((SGU_GROUPS, CHUNK, CHUNK), lambda i: (0, 0, 0)),
                  pl.BlockSpec((CHUNK, SGU_GROUPS), lambda i: (0, 0))],
        out_specs=pl.BlockSpec((CHUNK, SGU_W), lambda i: (i, 0)),
        out_shape=jax.ShapeDtypeStruct((M_TOK, SGU_W), BF16),
        compiler_params=_params(("arbitrary",), 32),
        name="sgu",
    )(big, big, norm_g.reshape(1, SGU_W), w_s.astype(BF16), b_s.T)


def _head_sink(sink_ref, idx):
    lane = lax.broadcasted_iota(jnp.int32, (1, LANES), 1)
    return jnp.sum(jnp.where(lane == idx, sink_ref[...], 0.0), axis=1, keepdims=True)


def _ctx_attn_kernel(q_ref, k_ref, v_ref, sink_ref, o_ref, ko_ref, vo_ref):
    kv = pl.program_id(1)
    scale = HD_A ** -0.5
    kb = k_ref[...]
    vb = v_ref[...]
    ko_ref[...] = kb.astype(F32)
    vo_ref[...] = vb.astype(F32)
    for g in range(GQA_G):
        sl = slice(g * HD_A, (g + 1) * HD_A)
        sk = _head_sink(sink_ref, kv * GQA_G + g)
        s = _dot_nt(q_ref[:, sl], kb) * scale
        m = jnp.maximum(jnp.max(s, axis=1, keepdims=True), sk)
        p = jnp.exp(s - m)
        den = jnp.sum(p, axis=1, keepdims=True) + jnp.exp(sk - m)
        o_ref[:, sl] = (_dot(p.astype(BF16), vb) / den).astype(o_ref.dtype)


def _ctx_attention(big, sink):
    qw = GQA_G * HD_A
    kv_spec = pl.BlockSpec((SEQ, HD_A), lambda b, kv: (b, kv))
    return pl.pallas_call(
        _ctx_attn_kernel,
        grid=(BATCH, KV_A),
        in_specs=[pl.BlockSpec((SEQ, qw), lambda b, kv: (b, OFF_AQ // qw + kv)),
                  pl.BlockSpec((SEQ, HD_A), lambda b, kv: (b, OFF_AK // HD_A + kv)),
                  pl.BlockSpec((SEQ, HD_A), lambda b, kv: (b, OFF_AV // HD_A + kv)),
                  pl.BlockSpec((1, LANES), lambda b, kv: (0, 0))],
        out_specs=[pl.BlockSpec((SEQ, qw), lambda b, kv: (b, kv)), kv_spec, kv_spec],
        out_shape=[jax.ShapeDtypeStruct((M_CTX, BRANCH_W), BF16),
                   jax.ShapeDtypeStruct((M_CTX, KV_A * HD_A), F32),
                   jax.ShapeDtypeStruct((M_CTX, KV_A * HD_A), F32)],
        compiler_params=_params(("arbitrary", "arbitrary"), 32),
        name="attn_ctx",
    )(big, big, big, sink)


def _lat_attn_kernel(q_ref, k_ref, v_ref, kc_ref, vc_ref, cos_ref, sin_ref, sink_ref, o_ref, kr_s):
    kv = pl.program_id(1)
    scale = HD_A ** -0.5
    lane = lax.broadcasted_iota(jnp.int32, (1, HD_A), 1)
    nf = ROPE_AXIS // 2
    first_half = (lane % ROPE_AXIS) < nf

    def rope(x, sl):
        swapped = jnp.where(first_half, pltpu.roll(x, HD_A - nf, 1), pltpu.roll(x, nf, 1))
        return x * cos_ref[sl, :] + swapped * sin_ref[sl, :]

    kr_s[...] = rope(k_ref[...].astype(F32), slice(None)).astype(BF16)
    kcb = kc_ref[0, 0].astype(BF16)
    vcb = vc_ref[0, 0].astype(BF16)
    nblk = DEC_SEQ // QBLK
    for j in range(nblk):
        lo = max(j - 1, 0) * QBLK
        hi = min(j + 2, nblk) * QBLK
        qs = slice(j * QBLK, (j + 1) * QBLK)
        kband = kr_s[lo:hi, :]
        vband = v_ref[lo:hi, :]
        qpos = j * QBLK + lax.broadcasted_iota(jnp.int32, (QBLK, hi - lo), 0)
        kpos = lo + lax.broadcasted_iota(jnp.int32, (QBLK, hi - lo), 1)
        mask = jnp.abs(kpos - qpos) <= WINDOW
        for g in range(GQA_G):
            sl = slice(g * HD_A, (g + 1) * HD_A)
            sk = _head_sink(sink_ref, kv * GQA_G + g)
            qg = rope(q_ref[qs, sl].astype(F32), qs).astype(BF16)
            s_band = jnp.where(mask, _dot_nt(qg, kband) * scale, -jnp.inf)
            s_ctx = _dot_nt(qg, kcb) * scale
            m = jnp.maximum(jnp.maximum(jnp.max(s_band, axis=1, keepdims=True),
                                        jnp.max(s_ctx, axis=1, keepdims=True)), sk)
            p_band = jnp.exp(s_band - m)
            p_ctx = jnp.exp(s_ctx - m)
            den = (jnp.sum(p_band, axis=1, keepdims=True) + jnp.sum(p_ctx, axis=1, keepdims=True)
                   + jnp.exp(sk - m))
            o = _dot(p_band.astype(BF16), vband) + _dot(p_ctx.astype(BF16), vcb)
            o_ref[qs, sl] = (o / den).astype(o_ref.dtype)


def _lat_attention(big, cache_k, cache_v, cos_t, sin_t, sink, layer):
    qw = GQA_G * HD_A
    r0 = M_CTX // DEC_SEQ
    return pl.pallas_call(
        _lat_attn_kernel,
        grid=(DEC_BATCH, KV_A),
        in_specs=[pl.BlockSpec((DEC_SEQ, qw), lambda b, kv: (r0 + b, OFF_AQ // qw + kv)),
                  pl.BlockSpec((DEC_SEQ, HD_A), lambda b, kv: (r0 + b, OFF_AK // HD_A + kv)),
                  pl.BlockSpec((DEC_SEQ, HD_A), lambda b, kv: (r0 + b, OFF_AV // HD_A + kv)),
                  pl.BlockSpec((1, 1, PAST_LEN, HD_A), lambda b, kv: (b, layer, 0, kv)),
                  pl.BlockSpec((1, 1, PAST_LEN, HD_A), lambda b, kv: (b, layer, 0, kv)),
                  pl.BlockSpec((DEC_SEQ, HD_A), lambda b, kv: (0, 0)),
                  pl.BlockSpec((DEC_SEQ, HD_A), lambda b, kv: (0, 0)),
                  pl.BlockSpec((1, LANES), lambda b, kv: (0, 0))],
        out_specs=pl.BlockSpec((DEC_SEQ, qw), lambda b, kv: (b, kv)),
        out_shape=jax.ShapeDtypeStruct((M_LAT, BRANCH_W), BF16),
        scratch_shapes=[pltpu.VMEM((DEC_SEQ, HD_A), BF16)],
        compiler_params=_params(("arbitrary", "arbitrary"), 32),
        name="attn_lat",
    )(big, big, big, cache_k, cache_v, cos_t, sin_t, sink)


def _rope_tables():
    rows = DEC_SEQ // GRID_W
    row = jnp.repeat(jnp.arange(rows, dtype=F32), GRID_W)
    col = jnp.tile(jnp.arange(GRID_W, dtype=F32), rows)
    nf = ROPE_AXIS // 2
    inv = ROPE_BASE ** (-jnp.arange(nf, dtype=F32) / nf)
    ar = row[:, None] * inv[None, :]
    ac = col[:, None] * inv[None, :]
    cr, sr, cc, sc = jnp.cos(ar), jnp.sin(ar), jnp.cos(ac), jnp.sin(ac)
    cos_t = jnp.concatenate([cr, cr, cc, cc], axis=1)
    sin_t = jnp.concatenate([-sr, sr, -sc, sc], axis=1)
    return cos_t, sin_t


def _pad_cols(a, width):
    return jnp.pad(a, ((0, 0), (0, width - a.shape[1])))


def kernel(x_prompt, x_sample, cache_k, cache_v, state_C, state_n, state_m, c, c_ctx, ada_w, ada_b, norm1_g, w_in, m_gate_b, m_norm_g, sgu_norm_g, sgu_w, sgu_b, attn_sink, w_br_m, w_br_s, w_br_a, w_out, norm2_g, ffn_up, ffn_conv_w, ffn_conv_b, ffn_down, final_g):
    cs = jnp.concatenate([c, c_ctx[None, :], jnp.zeros((MOD_ROWS - DEC_BATCH - 1, D_MODEL), F32)], axis=0)
    mod = _modulation(cs, ada_w, ada_b)
    w_down = _cast_pad_down(ffn_down)
    w_in_t = jnp.swapaxes(w_in, 1, 2)

    def pad_ffn(a):
        return jnp.pad(a, ((0, 0), (0, 0), (0, FFN_PAD - FFN_DIM)))

    conv_b = ffn_conv_b.reshape(DEPTH, 1, 2 * FFN_DIM)
    cwg, cwu = pad_ffn(ffn_conv_w[:, :, :FFN_DIM]), pad_ffn(ffn_conv_w[:, :, FFN_DIM:])
    cbg, cbu = pad_ffn(conv_b[:, :, :FFN_DIM]), pad_ffn(conv_b[:, :, FFN_DIM:])

    cos_t, sin_t = _rope_tables()
    cache_k2 = cache_k.reshape(DEC_BATCH, DEPTH, PAST_LEN, KV_A * HD_A)
    cache_v2 = cache_v.reshape(DEC_BATCH, DEPTH, PAST_LEN, KV_A * HD_A)
    st_n = state_n.reshape(DEC_BATCH, DEPTH, 2, H_M, 1, DK_M)
    st_m = jnp.broadcast_to(state_m[..., None, None], (DEC_BATCH, DEPTH, 2, H_M, 1, LANES))

    xs = (x_prompt.reshape(M_CTX, D_MODEL), x_sample.reshape(M_LAT, D_MODEL))
    ks, vs, cfs, nfs, mfs = [], [], [], [], []
    for l in range(DEPTH):
        sh1, sc1, g1, sh2, sc2, g2 = [mod[l, :, j * D_MODEL:(j + 1) * D_MODEL].reshape(MOD_ROWS, 1, D_MODEL)
                                      for j in range(6)]
        sink = _pad_cols(attn_sink[l].reshape(1, H_A), LANES)
        b_mg = m_gate_b[l].reshape(N_MG, 1)

        h = _norm_mod(xs, norm1_g[l], sc1, sh1)
        big = _in_proj(h, w_in_t, l)
        gpt = _gate_proj(h, w_in_t, b_mg, l)

        hm_c, cf, nf, mf = _mlstm(big, gpt, m_norm_g[l], l, latent=False)
        (hm_l,) = _mlstm(big, gpt, m_norm_g[l], l, latent=True, state=(state_C, st_n, st_m))
        hs = _sgu(big, sgu_norm_g[l], sgu_w[l], sgu_b[l])
        ha_c, k_new, v_new = _ctx_attention(big, sink)
        ha_l = _lat_attention(big, cache_k2, cache_v2, cos_t, sin_t, sink, l)

        y = _merge_proj(hm_c, hm_l, hs, ha_c, ha_l, w_br_m, w_br_s, w_br_a, big, l)
        x = _out_proj(y, w_out, l, xs, g1)

        h2 = _norm_mod((x,), norm2_g[l], sc2, sh2)
        u = _ffn_up(h2, ffn_up, cwg, cwu, cbg, cbu, l)
        x = _ffn_down(u, w_down, l, x, g2)
        xs = (x,)

        ks.append(k_new.reshape(BATCH, SEQ, KV_A, HD_A))
        vs.append(v_new.reshape(BATCH, SEQ, KV_A, HD_A))
        cfs.append(cf)
        nfs.append(nf.reshape(BATCH, 2, H_M, DK_M))
        mfs.append(mf[:, :, :, 0, 0])

    y_prompt = _final_norm(x, final_g, 0, M_CTX).reshape(BATCH, SEQ, D_MODEL)
    y_sample = _final_norm(x, final_g, M_CTX, M_LAT).reshape(DEC_BATCH, DEC_SEQ, D_MODEL)
    return (y_prompt, y_sample, jnp.stack(ks, axis=1), jnp.stack(vs, axis=1),
            jnp.stack(cfs, axis=1), jnp.stack(nfs, axis=1), jnp.stack(mfs, axis=1))
```

```python
import functools
import math

import numpy as np
import jax
import jax.numpy as jnp
from jax import lax
from jax.experimental import pallas as pl
from jax.experimental.pallas import tpu as pltpu

D_MODEL = 4096
BATCH = 32
SEQ = 256
DEPTH = 2
DEC_BATCH = 4
DEC_SEQ = 1024
PAST_LEN = 512
GRID_W = 64
BRANCH_W = D_MODEL // 2
CHUNK = 128
H_M = 8
DV_M = BRANCH_W // H_M
DK_M = DV_M // 2
SGU_W = BRANCH_W
SGU_GROUPS = 8
SGU_GW = SGU_W // SGU_GROUPS
H_A = 16
KV_A = 4
HD_A = BRANCH_W // H_A
GQA_G = H_A // KV_A
WINDOW = 128
QBLK = 128
ROPE_BASE = 10000.0
ROPE_AXIS = HD_A // 2
FFN_DIM = 11008
EPS = 1e-6

F32 = jnp.float32
BF16 = jnp.bfloat16

M_CTX = BATCH * SEQ
M_LAT = DEC_BATCH * DEC_SEQ
M_TOK = M_CTX + M_LAT
MOD_ROWS = 8
CTX_MOD_ROW = DEC_BATCH

LANES = 128
MIB = 1024 * 1024

N_MG = 4 * H_M
OFF_MQ = 0
OFF_MK = OFF_MQ + H_M * DK_M
OFF_MV = OFF_MK + H_M * DK_M
OFF_MO = OFF_MV + BRANCH_W
N_A = OFF_MO + BRANCH_W
OFF_SU = N_A
OFF_SV = OFF_SU + SGU_W
OFF_AQ = OFF_SV + SGU_W
OFF_AK = OFF_AQ + H_A * HD_A
OFF_AV = OFF_AK + KV_A * HD_A
OFF_GATES = OFF_AV + KV_A * HD_A
N_MAIN = OFF_GATES + 3 * D_MODEL

FFN_TILE = 256
FFN_PAD = 11264
FFN_REAL_TILES = FFN_DIM // FFN_TILE
ROW_TILE = 1024
W_TILE = 512

_GELU_C = float(np.float32(math.sqrt(2.0 / math.pi)))


def _params(sem, vmem_mib):
    return pltpu.CompilerParams(dimension_semantics=sem, vmem_limit_bytes=vmem_mib * MIB)


def _mod_row(i, bm):
    n_ctx = M_CTX // bm
    per_batch = DEC_SEQ // bm
    return jnp.where(i < n_ctx, CTX_MOD_ROW, (i - n_ctx) // per_batch)


def _ctx_tile(i, bm):
    return jnp.minimum(i, M_CTX // bm - 1)


def _lat_tile(i, bm):
    return jnp.maximum(i - M_CTX // bm, 0)


def _dot(a, b):
    return jnp.dot(a, b, preferred_element_type=F32)


def _dot_nt(a, b):
    return lax.dot_general(a, b, (((1,), (1,)), ((), ())), preferred_element_type=F32)


def _gelu(x):
    return 0.5 * x * (1.0 + jnp.tanh(_GELU_C * (x + 0.044715 * (x * x * x))))


def _log_sigmoid(x):
    return jnp.minimum(x, 0.0) - jnp.log(1.0 + jnp.exp(-jnp.abs(x)))


def _run_interleaved(gens):
    out = [None] * len(gens)
    live = list(enumerate(gens))
    while live:
        still = []
        for idx, gen in live:
            try:
                next(gen)
                still.append((idx, gen))
            except StopIteration as done:
                out[idx] = done.value
        live = still
    return out


def _mod_kernel(c_ref, w_ref, b_ref, o_ref):
    c = c_ref[...]
    s = (c * jax.nn.sigmoid(c)).astype(BF16)
    o_ref[0] = _dot(s, w_ref[0].astype(BF16)) + b_ref[0]


def _modulation(cs, ada_w, ada_b):
    bn = 512
    n = ada_w.shape[-1]
    return pl.pallas_call(
        _mod_kernel,
        grid=(DEPTH, n // bn),
        in_specs=[pl.BlockSpec((MOD_ROWS, D_MODEL), lambda l, j: (0, 0)),
                  pl.BlockSpec((1, D_MODEL, bn), lambda l, j: (l, 0, j)),
                  pl.BlockSpec((1, 1, bn), lambda l, j: (l, 0, j))],
        out_specs=pl.BlockSpec((1, MOD_ROWS, bn), lambda l, j: (l, 0, j)),
        out_shape=jax.ShapeDtypeStruct((DEPTH, MOD_ROWS, n), F32),
        compiler_params=_params(("arbitrary", "arbitrary"), 40),
        name="adaln_mod",
    )(cs, ada_w, ada_b.reshape(DEPTH, 1, n))


def _norm_mod_body(x, g_ref, sc_ref, sh_ref, o_ref):
    ms = jnp.mean(x * x, axis=-1, keepdims=True)
    y = x * lax.rsqrt(ms + EPS) * g_ref[...]
    o_ref[...] = (y * (1.0 + sc_ref[0]) + sh_ref[0]).astype(o_ref.dtype)


def _norm_mod_kernel(x_ref, g_ref, sc_ref, sh_ref, o_ref):
    _norm_mod_body(x_ref[...], g_ref, sc_ref, sh_ref, o_ref)


def _norm_mod_split_kernel(xc_ref, xl_ref, g_ref, sc_ref, sh_ref, o_ref, *, bm):
    i = pl.program_id(0)

    @pl.when(i < M_CTX // bm)
    def _():
        _norm_mod_body(xc_ref[...], g_ref, sc_ref, sh_ref, o_ref)

    @pl.when(i >= M_CTX // bm)
    def _():
        _norm_mod_body(xl_ref[...], g_ref, sc_ref, sh_ref, o_ref)


def _norm_mod(xs, g, sc, sh):
    bm = 512
    mod_spec = pl.BlockSpec((1, 1, D_MODEL), lambda i: (_mod_row(i, bm), 0, 0))
    if len(xs) == 1:
        kern = _norm_mod_kernel
        x_specs = [pl.BlockSpec((bm, D_MODEL), lambda i: (i, 0))]
    else:
        kern = functools.partial(_norm_mod_split_kernel, bm=bm)
        x_specs = [pl.BlockSpec((bm, D_MODEL), lambda i: (_ctx_tile(i, bm), 0)),
                   pl.BlockSpec((bm, D_MODEL), lambda i: (_lat_tile(i, bm), 0))]
    return pl.pallas_call(
        kern,
        grid=(M_TOK // bm,),
        in_specs=x_specs + [pl.BlockSpec((1, D_MODEL), lambda i: (0, 0)), mod_spec, mod_spec],
        out_specs=pl.BlockSpec((bm, D_MODEL), lambda i: (i, 0)),
        out_shape=jax.ShapeDtypeStruct((M_TOK, D_MODEL), BF16),
        compiler_params=_params(("arbitrary",), 48),
        name="norm_mod",
    )(*xs, g.reshape(1, D_MODEL), sc, sh)


def _final_norm_kernel(x_ref, g_ref, o_ref):
    x = x_ref[...]
    ms = jnp.mean(x * x, axis=-1, keepdims=True)
    o_ref[...] = x * lax.rsqrt(ms + EPS) * g_ref[...]


def _final_norm(x, g, row0, rows):
    bm = 256
    b0 = row0 // bm
    return pl.pallas_call(
        _final_norm_kernel,
        grid=(rows // bm,),
        in_specs=[pl.BlockSpec((bm, D_MODEL), lambda i: (b0 + i, 0)),
                  pl.BlockSpec((1, D_MODEL), lambda i: (0, 0))],
        out_specs=pl.BlockSpec((bm, D_MODEL), lambda i: (i, 0)),
        out_shape=jax.ShapeDtypeStruct((rows, D_MODEL), F32),
        compiler_params=_params(("arbitrary",), 32),
        name="final_norm",
    )(x, g.reshape(1, D_MODEL))


W_CHUNKS = 8

def _in_proj_kernel(x_ref, wt_ref, o_ref, wb_ref, buf_ref, sem, *, layer, bn, n_tiles):
    j = pl.program_id(0)
    i = pl.program_id(1)
    ch = bn // W_CHUNKS

    def chunk_copy(tile, c):
        row0 = tile * bn + jnp.where(tile >= N_A // bn, N_MG, 0) + c * ch
        return pltpu.make_async_copy(wt_ref.at[layer, pl.ds(pl.multiple_of(row0, 8), ch), :], buf_ref, sem)

    def round_chunk(slot, c):
        wb_ref[slot, pl.ds(pl.multiple_of(c * ch, ch), ch), :] = buf_ref[...].astype(BF16)

    @pl.when(jnp.logical_and(j == 0, i == 0))
    def _():
        def body(c, carry):
            cp = chunk_copy(0, c)
            cp.start()
            cp.wait()
            round_chunk(0, c)
            return carry

        lax.fori_loop(0, W_CHUNKS, body, 0)

    prefetch = jnp.logical_and(i < W_CHUNKS, j + 1 < n_tiles)

    @pl.when(prefetch)
    def _():
        chunk_copy(j + 1, i).start()

    o_ref[...] = _dot_nt(x_ref[...], wb_ref[j % 2]).astype(o_ref.dtype)

    @pl.when(prefetch)
    def _():
        chunk_copy(j + 1, i).wait()
        round_chunk((j + 1) % 2, i)


def _in_proj(h, w_in_t, layer):
    bm, bn = ROW_TILE, 1024
    n_tiles = N_MAIN // bn
    return pl.pallas_call(
        functools.partial(_in_proj_kernel, layer=layer, bn=bn, n_tiles=n_tiles),
        grid=(n_tiles, M_TOK // bm),
        in_specs=[pl.BlockSpec((bm, D_MODEL), lambda j, i: (i, 0)),
                  pl.BlockSpec(memory_space=pl.ANY)],
        out_specs=pl.BlockSpec((bm, bn), lambda j, i: (i, j)),
        out_shape=jax.ShapeDtypeStruct((M_TOK, N_MAIN), BF16),
        scratch_shapes=[pltpu.VMEM((2, bn, D_MODEL), BF16),
                        pltpu.VMEM((bn // W_CHUNKS, D_MODEL), F32),
                        pltpu.SemaphoreType.DMA(())],
        compiler_params=_params(("arbitrary", "arbitrary"), 48),
        name="in_proj",
    )(h, w_in_t)


def _gate_kernel(x_ref, w_ref, b_ref, o_ref, *, bm):
    g = _dot_nt(w_ref[0].astype(BF16), x_ref[...]) + b_ref[...]
    row = lax.broadcasted_iota(jnp.int32, (N_MG, CHUNK), 0)
    fwd_forget = jnp.logical_and(row >= H_M, row < 2 * H_M)
    bwd_forget = row >= 3 * H_M
    for c in range(bm // CHUNK):
        blk = g[:, c * CHUNK:(c + 1) * CHUNK]
        ls = _log_sigmoid(blk)
        o_ref[:, c * CHUNK:(c + 1) * CHUNK] = jnp.where(
            fwd_forget, _lane_cumsum(ls, reverse=False),
            jnp.where(bwd_forget, _lane_cumsum(ls, reverse=True), blk))


def _gate_proj(h, w_in_t, b_mg, layer):
    bm = ROW_TILE
    return pl.pallas_call(
        functools.partial(_gate_kernel, bm=bm),
        grid=(M_TOK // bm,),
        in_specs=[pl.BlockSpec((bm, D_MODEL), lambda i: (i, 0)),
                  pl.BlockSpec((1, N_MG, D_MODEL), lambda i: (layer, N_A // N_MG, 0)),
                  pl.BlockSpec((N_MG, 1), lambda i: (0, 0))],
        out_specs=pl.BlockSpec((N_MG, bm), lambda i: (0, i)),
        out_shape=jax.ShapeDtypeStruct((N_MG, M_TOK), F32),
        compiler_params=_params(("arbitrary",), 32),
        name="gate_proj",
    )(h, w_in_t, b_mg)


def _out_proj_kernel(*refs, bm, split):
    if split:
        x_ref, w_ref, rc_ref, rl_ref, g_ref, o_ref, wb_ref = refs
    else:
        x_ref, w_ref, r_ref, g_ref, o_ref, wb_ref = refs
    i = pl.program_id(1)

    @pl.when(i == 0)
    def _():
        wb_ref[...] = w_ref[0].astype(BF16)

    y = g_ref[0] * _dot(x_ref[...], wb_ref[...])
    if split:
        @pl.when(i < M_CTX // bm)
        def _():
            o_ref[...] = rc_ref[...] + y

        @pl.when(i >= M_CTX // bm)
        def _():
            o_ref[...] = rl_ref[...] + y
    else:
        o_ref[...] = r_ref[...] + y


def _out_proj(y, w_out, layer, res, gate):
    bm, bn = ROW_TILE, W_TILE
    split = len(res) == 2
    if split:
        r_specs = [pl.BlockSpec((bm, bn), lambda j, i: (_ctx_tile(i, bm), j)),
                   pl.BlockSpec((bm, bn), lambda j, i: (_lat_tile(i, bm), j))]
    else:
        r_specs = [pl.BlockSpec((bm, bn), lambda j, i: (i, j))]
    return pl.pallas_call(
        functools.partial(_out_proj_kernel, bm=bm, split=split),
        grid=(D_MODEL // bn, M_TOK // bm),
        in_specs=[pl.BlockSpec((bm, D_MODEL), lambda j, i: (i, 0)),
                  pl.BlockSpec((1, D_MODEL, bn), lambda j, i: (layer, 0, j))] + r_specs + [
                  pl.BlockSpec((1, 1, bn), lambda j, i: (_mod_row(i, bm), 0, j))],
        out_specs=pl.BlockSpec((bm, bn), lambda j, i: (i, j)),
        out_shape=jax.ShapeDtypeStruct((M_TOK, D_MODEL), F32),
        scratch_shapes=[pltpu.VMEM((D_MODEL, bn), BF16)],
        compiler_params=_params(("arbitrary", "arbitrary"), 52),
        name="out_proj",
    )(y, w_out, *res, gate)


def _merge_kernel(hmc_ref, hml_ref, hs_ref, hac_ref, hal_ref, wm_ref, ws_ref, wa_ref,
                  gm_ref, gs_ref, ga_ref, o_ref, wmb_ref, wsb_ref, wab_ref, *, bm):
    i = pl.program_id(1)

    @pl.when(i == 0)
    def _():
        wmb_ref[...] = wm_ref[0].astype(BF16)
        wsb_ref[...] = ws_ref[0].astype(BF16)
        wab_ref[...] = wa_ref[0].astype(BF16)

    def gate(ref):
        return jax.nn.sigmoid(ref[...].astype(F32))

    def finish(hm, ha):
        dm = _dot(hm, wmb_ref[...])
        ds = _dot(hs_ref[...], wsb_ref[...])
        da = _dot(ha, wab_ref[...])
        y = gate(gm_ref) * dm + gate(gs_ref) * ds + gate(ga_ref) * da
        o_ref[...] = y.astype(o_ref.dtype)

    @pl.when(i < M_CTX // bm)
    def _():
        finish(hmc_ref[...], hac_ref[...])

    @pl.when(i >= M_CTX // bm)
    def _():
        finish(hml_ref[...], hal_ref[...])


def _merge_proj(hm_c, hm_l, hs, ha_c, ha_l, wm, ws, wa, big, layer):
    bm, bn = 256, 1024
    gate0 = OFF_GATES // bn
    per = D_MODEL // bn
    cspec = pl.BlockSpec((bm, BRANCH_W), lambda j, i: (_ctx_tile(i, bm), 0))
    lspec = pl.BlockSpec((bm, BRANCH_W), lambda j, i: (_lat_tile(i, bm), 0))
    xspec = pl.BlockSpec((bm, BRANCH_W), lambda j, i: (i, 0))
    wspec = pl.BlockSpec((1, BRANCH_W, bn), lambda j, i: (layer, 0, j), pipeline_mode=pl.Buffered(1))

    def gspec(b):
        return pl.BlockSpec((bm, bn), lambda j, i: (i, gate0 + b * per + j))

    return pl.pallas_call(
        functools.partial(_merge_kernel, bm=bm),
        grid=(D_MODEL // bn, M_TOK // bm),
        in_specs=[cspec, lspec, xspec, cspec, lspec, wspec, wspec, wspec, gspec(0), gspec(1), gspec(2)],
        out_specs=pl.BlockSpec((bm, bn), lambda j, i: (i, j)),
        out_shape=jax.ShapeDtypeStruct((M_TOK, D_MODEL), BF16),
        scratch_shapes=[pltpu.VMEM((BRANCH_W, bn), BF16)] * 3,
        compiler_params=_params(("arbitrary", "arbitrary"), 56),
        name="merge_proj",
    )(hm_c, hm_l, hs, ha_c, ha_l, wm, ws, wa, big, big, big)


def _ffn_up_kernel(x_ref, w_ref, cwg_ref, cwu_ref, cbg_ref, cbu_ref, o_ref,
                   wg_ref, wu_ref, bufg_ref, bufu_ref, sems, *, layer, bm, bn, n_tiles):
    j = pl.program_id(0)
    i = pl.program_id(1)
    ch = D_MODEL // W_CHUNKS
    tail = FFN_DIM - (n_tiles - 1) * bn

    def rows(c):
        return pl.ds(pl.multiple_of(c * ch, ch), ch)

    def gate_copy(tile, c):
        cols = pl.ds(pl.multiple_of(tile * bn, LANES), bn)
        return pltpu.make_async_copy(w_ref.at[layer, rows(c), cols], bufg_ref, sems.at[0])

    def up_copy(tile, c):
        cols = pl.ds(pl.multiple_of(FFN_DIM + tile * bn, LANES), bn)
        return pltpu.make_async_copy(w_ref.at[layer, rows(c), cols], bufu_ref, sems.at[1])

    def up_tail_copy(c):
        cols = pl.ds(FFN_DIM + (n_tiles - 1) * bn, tail)
        return pltpu.make_async_copy(w_ref.at[layer, rows(c), cols], bufu_ref.at[:, pl.ds(0, tail)], sems.at[1])

    def round_chunks(slot, c, is_tail):
        wg_ref[slot, rows(c), :] = bufg_ref[...].astype(BF16)
        if is_tail:
            wu_ref[slot, rows(c), 0:tail] = bufu_ref[:, 0:tail].astype(BF16)
            wu_ref[slot, rows(c), tail:bn] = jnp.zeros((ch, bn - tail), BF16)
        else:
            wu_ref[slot, rows(c), :] = bufu_ref[...].astype(BF16)

    @pl.when(jnp.logical_and(j == 0, i == 0))
    def _():
        def body(c, carry):
            cg, cu = gate_copy(0, c), up_copy(0, c)
            cg.start()
            cu.start()
            cg.wait()
            cu.wait()
            round_chunks(0, c, False)
            return carry

        lax.fori_loop(0, W_CHUNKS, body, 0)

    prefetch = jnp.logical_and(i < W_CHUNKS, j + 1 < n_tiles)
    to_tail = j + 1 == n_tiles - 1
    pre_full = jnp.logical_and(prefetch, jnp.logical_not(to_tail))
    pre_tail = jnp.logical_and(prefetch, to_tail)

    @pl.when(pre_full)
    def _():
        gate_copy(j + 1, i).start()
        up_copy(j + 1, i).start()

    @pl.when(pre_tail)
    def _():
        gate_copy(j + 1, i).start()
        up_tail_copy(i).start()

    tlen = jnp.where(i < M_CTX // bm, SEQ, DEC_SEQ)
    tpos = lax.broadcasted_iota(jnp.int32, (bm, 1), 0) & (tlen - 1)
    first = tpos == 0
    last = tpos == tlen - 1
    x = x_ref[...]

    def conv(a, cw_ref, cb_ref):
        prev = jnp.where(first, 0.0, pltpu.roll(a, 1, 0))
        nxt = jnp.where(last, 0.0, pltpu.roll(a, bm - 1, 0))
        return cw_ref[0, 0:1, :] * prev + cw_ref[0, 1:2, :] * a + cw_ref[0, 2:3, :] * nxt + cb_ref[0]

    ag = _dot(x, wg_ref[j % 2])
    au = _dot(x, wu_ref[j % 2])
    g = conv(ag, cwg_ref, cbg_ref)
    u = conv(au, cwu_ref, cbu_ref)
    o_ref[...] = (g * jax.nn.sigmoid(g) * u).astype(o_ref.dtype)

    @pl.when(pre_full)
    def _():
        gate_copy(j + 1, i).wait()
        up_copy(j + 1, i).wait()
        round_chunks((j + 1) % 2, i, False)

    @pl.when(pre_tail)
    def _():
        gate_copy(j + 1, i).wait()
        up_tail_copy(i).wait()
        round_chunks((j + 1) % 2, i, True)


def _ffn_up(h2, ffn_up_w, cwg, cwu, cbg, cbu, layer):
    bm, bn = ROW_TILE, 512
    n_tiles = FFN_PAD // bn
    cw_spec = pl.BlockSpec((1, 3, bn), lambda j, i: (layer, 0, j))
    cb_spec = pl.BlockSpec((1, 1, bn), lambda j, i: (layer, 0, j))
    return pl.pallas_call(
        functools.partial(_ffn_up_kernel, layer=layer, bm=bm, bn=bn, n_tiles=n_tiles),
        grid=(n_tiles, M_TOK // bm),
        in_specs=[pl.BlockSpec((bm, D_MODEL), lambda j, i: (i, 0)),
                  pl.BlockSpec(memory_space=pl.ANY),
                  cw_spec, cw_spec, cb_spec, cb_spec],
        out_specs=pl.BlockSpec((bm, bn), lambda j, i: (i, j)),
        out_shape=jax.ShapeDtypeStruct((M_TOK, FFN_PAD), BF16),
        scratch_shapes=[pltpu.VMEM((2, D_MODEL, bn), BF16), pltpu.VMEM((2, D_MODEL, bn), BF16),
                        pltpu.VMEM((D_MODEL // W_CHUNKS, bn), F32), pltpu.VMEM((D_MODEL // W_CHUNKS, bn), F32),
                        pltpu.SemaphoreType.DMA((2,))],
        compiler_params=_params(("arbitrary", "arbitrary"), 56),
        name="ffn_up",
    )(h2, ffn_up_w, cwg, cwu, cbg, cbu)


def _cast_pad_kernel(w_ref, o_ref):
    j = pl.program_id(1)

    @pl.when(j < FFN_REAL_TILES)
    def _():
        o_ref[0] = w_ref[0].astype(BF16)

    @pl.when(j >= FFN_REAL_TILES)
    def _():
        o_ref[...] = jnp.zeros_like(o_ref)


def _cast_pad_down(ffn_down_w):
    br = FFN_TILE
    return pl.pallas_call(
        _cast_pad_kernel,
        grid=(DEPTH, FFN_PAD // br),
        in_specs=[pl.BlockSpec((1, br, D_MODEL), lambda l, j: (l, jnp.minimum(j, FFN_REAL_TILES - 1), 0))],
        out_specs=pl.BlockSpec((1, br, D_MODEL), lambda l, j: (l, j, 0)),
        out_shape=jax.ShapeDtypeStruct((DEPTH, FFN_PAD, D_MODEL), BF16),
        compiler_params=_params(("arbitrary", "arbitrary"), 32),
        name="cast_down_w",
    )(ffn_down_w)


def _ffn_down_kernel(x_ref, w_ref, r_ref, g_ref, o_ref, acc_ref, *, nk):
    k = pl.program_id(2)

    @pl.when(k == 0)
    def _():
        acc_ref[...] = jnp.zeros_like(acc_ref)

    acc_ref[...] += _dot(x_ref[...], w_ref[0])

    @pl.when(k == nk - 1)
    def _():
        o_ref[...] = r_ref[...] + g_ref[0] * acc_ref[...]


def _ffn_down(u, w, layer, res, gate):
    bm, bn, bk = ROW_TILE, 1024, FFN_PAD // 4
    nk = FFN_PAD // bk
    return pl.pallas_call(
        functools.partial(_ffn_down_kernel, nk=nk),
        grid=(D_MODEL // bn, M_TOK // bm, nk),
        in_specs=[pl.BlockSpec((bm, bk), lambda j, i, k: (i, k)),
                  pl.BlockSpec((1, bk, bn), lambda j, i, k: (layer, k, j)),
                  pl.BlockSpec((bm, bn), lambda j, i, k: (i, j)),
                  pl.BlockSpec((1, 1, bn), lambda j, i, k: (_mod_row(i, bm), 0, j))],
        out_specs=pl.BlockSpec((bm, bn), lambda j, i, k: (i, j)),
        out_shape=jax.ShapeDtypeStruct((M_TOK, D_MODEL), F32),
        scratch_shapes=[pltpu.VMEM((bm, bn), F32)],
        compiler_params=_params(("arbitrary", "arbitrary", "arbitrary"), 52),
        name="ffn_down",
    )(u, w, res, gate)


def _lane_cumsum(x, reverse):
    lane = lax.broadcasted_iota(jnp.int32, x.shape, 1)
    d = 1
    while d < LANES:
        if reverse:
            x = x + jnp.where(lane < LANES - d, pltpu.roll(x, LANES - d, 1), 0.0)
        else:
            x = x + jnp.where(lane >= d, pltpu.roll(x, d, 1), 0.0)
        d *= 2
    return x


def _mlstm_kernel(*refs, seq, hp, with_init, emit_state):
    refs = list(refs)
    q_ref, k_ref, v_ref, mo_ref, gpt_ref, gain_ref = refs[:6]
    pos = 6
    if with_init:
        c0_ref, n0_ref, m0_ref = refs[pos:pos + 3]
        pos += 3
    hm_ref = refs[pos]
    pos += 1
    if emit_state:
        cf_ref, nf_ref, mf_ref = refs[pos:pos + 3]
        pos += 3
    hf_s, hb_s = refs[pos:pos + 2]

    nc = seq // CHUNK
    scale = DK_M ** -0.5

    def head_gates(head):
        def gate_rows(j):
            sub = lax.broadcasted_iota(jnp.int32, (H_M, CHUNK), 0)
            parts = [jnp.sum(jnp.where(sub == head, gpt_ref[j * H_M:(j + 1) * H_M, c * CHUNK:(c + 1) * CHUNK], 0.0),
                             axis=0, keepdims=True) for c in range(nc)]
            if nc < 8:
                parts.append(jnp.zeros((8 - nc, CHUNK), F32))
            return jnp.concatenate(parts, axis=0)

        i_f, b_f, i_b, b_b = [gate_rows(j) for j in range(4)]
        rows = jnp.concatenate([b_f, i_f, b_b, i_b, jnp.zeros((LANES - 32, CHUNK), F32)], axis=0)
        return rows, rows.T

    t_idx = lax.broadcasted_iota(jnp.int32, (CHUNK, CHUNK), 0)
    s_idx = lax.broadcasted_iota(jnp.int32, (CHUNK, CHUNK), 1)
    causal = s_idx <= t_idx
    anti = s_idx >= t_idx

    def chunk_step(hh, gates, c, backward, state, h_scr):
        rows, cols = gates
        cmat, nvec, mval = state
        r0 = c * CHUNK
        base = 16 if backward else 0
        b_row = rows[base + c:base + c + 1, :]
        i_row = rows[base + 8 + c:base + 9 + c, :]
        b_col = cols[:, base + c:base + c + 1]
        i_col = cols[:, base + 8 + c:base + 9 + c]
        end = 0 if backward else CHUNK - 1

        qb = q_ref[r0:r0 + CHUNK, hh * DK_M:(hh + 1) * DK_M]
        kb = k_ref[r0:r0 + CHUNK, hh * DK_M:(hh + 1) * DK_M]
        vb = v_ref[r0:r0 + CHUNK, hh * DV_M:(hh + 1) * DV_M]

        dmat = jnp.where(anti if backward else causal, b_col + (i_row - b_row), -jnp.inf)
        g = b_col + mval
        mt = jnp.maximum(g, jnp.max(dmat, axis=1, keepdims=True))
        s_qk = _dot_nt(qb, kb)
        q_c = _dot(qb, cmat.astype(BF16))
        yield
        w_inter = jnp.exp(g - mt)
        a = jnp.exp(dmat - mt) * (s_qk * scale)
        m_new = mt[end:end + 1, :]
        w_s = jnp.exp(b_col[end:end + 1, :] - b_col + i_col - m_new)
        decay = jnp.exp(g[end:end + 1, :] - m_new)
        yield
        num = w_inter * (q_c * scale) + _dot(a.astype(BF16), vb)
        qn = jnp.sum(qb.astype(F32) * nvec, axis=1, keepdims=True) * scale
        den = w_inter * qn + jnp.sum(a, axis=1, keepdims=True)
        kw = kb.astype(F32) * w_s
        yield
        h_scr[r0:r0 + CHUNK, hh * DV_M:(hh + 1) * DV_M] = num / jnp.maximum(jnp.abs(den), jnp.exp(-mt))
        c_new = decay * cmat + _dot(kw.T.astype(BF16), vb)
        n_new = decay * nvec + jnp.sum(kw, axis=0, keepdims=True)
        return c_new, n_new, m_new


    def init(hh, d):
        if with_init:
            return c0_ref[0, 0, d, hh], n0_ref[0, 0, d, hh], m0_ref[0, 0, d, hh][:, 0:1]
        return jnp.zeros((DK_M, DV_M), F32), jnp.zeros((1, DK_M), F32), jnp.zeros((1, 1), F32)

    gates = [head_gates(pl.program_id(1) * hp + hh) for hh in range(hp)]
    states = [[init(hh, 0), init(hh, 1)] for hh in range(hp)]
    for c in range(nc):
        steps = []
        for hh in range(hp):
            steps.append(chunk_step(hh, gates[hh], c, False, states[hh][0], hf_s))
            steps.append(chunk_step(hh, gates[hh], nc - 1 - c, True, states[hh][1], hb_s))
        new = _run_interleaved(steps)
        for hh in range(hp):
            states[hh] = [new[2 * hh], new[2 * hh + 1]]

    if emit_state:
        for hh in range(hp):
            for d in range(2):
                cf_ref[0, d, hh] = states[hh][d][0]
                nf_ref[0, d, hh] = states[hh][d][1]
                mf_ref[0, d, hh] = jnp.broadcast_to(states[hh][d][2], (1, LANES))

    for hh in range(hp):
        hs = slice(hh * DV_M, (hh + 1) * DV_M)
        gain = gain_ref[hh]
        for c in range(nc):
            sl = slice(c * CHUNK, (c + 1) * CHUNK)
            hsum = hf_s[sl, hs] + hb_s[sl, hs]
            ms = jnp.mean(hsum * hsum, axis=-1, keepdims=True)
            y = hsum * lax.rsqrt(ms + EPS) * gain
            hm_ref[sl, hs] = (y * jax.nn.sigmoid(mo_ref[sl, hs].astype(F32))).astype(hm_ref.dtype)


def _mlstm(big, gpt, gain, layer, *, latent, state=None):
    seq = DEC_SEQ if latent else SEQ
    nb = DEC_BATCH if latent else BATCH
    hp = 4 if latent else 8
    r0 = (M_CTX // seq) if latent else 0
    with_init = latent
    emit_state = not latent
    qk_w, v_w = hp * DK_M, hp * DV_M

    in_specs = [
        pl.BlockSpec((seq, qk_w), lambda b, h: (r0 + b, OFF_MQ // qk_w + h)),
        pl.BlockSpec((seq, qk_w), lambda b, h: (r0 + b, OFF_MK // qk_w + h)),
        pl.BlockSpec((seq, v_w), lambda b, h: (r0 + b, OFF_MV // v_w + h)),
        pl.BlockSpec((seq, v_w), lambda b, h: (r0 + b, OFF_MO // v_w + h)),
        pl.BlockSpec((N_MG, seq), lambda b, h: (0, r0 + b)),
        pl.BlockSpec((hp, 1, DV_M), lambda b, h: (h, 0, 0)),
    ]
    args = [big, big, big, big, gpt, gain.reshape(H_M, 1, DV_M)]
    if with_init:
        st_c, st_n, st_m = state
        in_specs += [
            pl.BlockSpec((1, 1, 2, hp, DK_M, DV_M), lambda b, h: (b, layer, 0, h, 0, 0)),
            pl.BlockSpec((1, 1, 2, hp, 1, DK_M), lambda b, h: (b, layer, 0, h, 0, 0)),
            pl.BlockSpec((1, 1, 2, hp, 1, LANES), lambda b, h: (b, layer, 0, h, 0, 0)),
        ]
        args += [st_c, st_n, st_m]
    out_specs = [pl.BlockSpec((seq, v_w), lambda b, h: (b, h))]
    out_shape = [jax.ShapeDtypeStruct((nb * seq, BRANCH_W), BF16)]
    if emit_state:
        out_specs += [
            pl.BlockSpec((1, 2, hp, DK_M, DV_M), lambda b, h: (b, 0, h, 0, 0)),
            pl.BlockSpec((1, 2, hp, 1, DK_M), lambda b, h: (b, 0, h, 0, 0)),
            pl.BlockSpec((1, 2, hp, 1, LANES), lambda b, h: (b, 0, h, 0, 0)),
        ]
        out_shape += [
            jax.ShapeDtypeStruct((nb, 2, H_M, DK_M, DV_M), F32),
            jax.ShapeDtypeStruct((nb, 2, H_M, 1, DK_M), F32),
            jax.ShapeDtypeStruct((nb, 2, H_M, 1, LANES), F32),
        ]
    return pl.pallas_call(
        functools.partial(_mlstm_kernel, seq=seq, hp=hp, with_init=with_init, emit_state=emit_state),
        grid=(nb, H_M // hp),
        in_specs=in_specs,
        out_specs=out_specs,
        out_shape=out_shape,
        scratch_shapes=[pltpu.VMEM((seq, v_w), F32), pltpu.VMEM((seq, v_w), F32)],
        compiler_params=_params(("arbitrary", "arbitrary"), 40),
        name="mlstm_lat" if latent else "mlstm_ctx",
    )(*args)


def _sgu_kernel(su_ref, sv_ref, g_ref, w_ref, b_ref, o_ref):
    v = _gelu(sv_ref[...].astype(F32))
    ms = jnp.mean(v * v, axis=-1, keepdims=True)
    vn = (v * lax.rsqrt(ms + EPS) * g_ref[...]).astype(BF16)
    def group(g):
        sl = slice(g * SGU_GW, (g + 1) * SGU_GW)
        mixed = _dot(w_ref[g], vn[:, sl]) + b_ref[:, g:g + 1]
        yield
        o_ref[:, sl] = (_gelu(su_ref[:, sl].astype(F32)) * mixed).astype(o_ref.dtype)

    _run_interleaved([group(g) for g in range(SGU_GROUPS)])


def _sgu(big, norm_g, w_s, b_s):
    return pl.pallas_call(
        _sgu_kernel,
        grid=(M_TOK // CHUNK,),
        in_specs=[pl.BlockSpec((CHUNK, SGU_W), lambda i: (i, OFF_SU // SGU_W)),
                  pl.BlockSpec((CHUNK, SGU_W), lambda i: (i, OFF_SV // SGU_W)),
                  pl.BlockSpec((1, SGU_W), lambda i: (0, 0)),
                  pl.BlockSpec((SGU_GROUPS, CHUNK, CHUNK), lambda i: (0, 0, 0)),
                  pl.BlockSpec((CHUNK, SGU_GROUPS), lambda i: (0, 0))],
        out_specs=pl.BlockSpec((CHUNK, SGU_W), lambda i: (i, 0)),
        out_shape=jax.ShapeDtypeStruct((M_TOK, SGU_W), BF16),
        compiler_params=_params(("arbitrary",), 32),
        name="sgu",
    )(big, big, norm_g.reshape(1, SGU_W), w_s.astype(BF16), b_s.T)


def _head_sink(sink_ref, idx):
    lane = lax.broadcasted_iota(jnp.int32, (1, LANES), 1)
    return jnp.sum(jnp.where(lane == idx, sink_ref[...], 0.0), axis=1, keepdims=True)


def _ctx_attn_kernel(q_ref, k_ref, v_ref, sink_ref, o_ref, ko_ref, vo_ref):
    kv = pl.program_id(1)
    scale = HD_A ** -0.5
    kb = k_ref[...]
    vb = v_ref[...]
    ko_ref[...] = kb.astype(F32)
    vo_ref[...] = vb.astype(F32)
    def head(g):
        sl = slice(g * HD_A, (g + 1) * HD_A)
        sk = _head_sink(sink_ref, kv * GQA_G + g)
        s = _dot_nt(q_ref[:, sl], kb) * scale
        yield
        m = jnp.maximum(jnp.max(s, axis=1, keepdims=True), sk)
        p = jnp.exp(s - m)
        yield
        den = jnp.sum(p, axis=1, keepdims=True) + jnp.exp(sk - m)
        o = _dot(p.astype(BF16), vb)
        yield
        o_ref[:, sl] = (o / den).astype(o_ref.dtype)

    _run_interleaved([head(g) for g in range(GQA_G)])


def _ctx_attention(big, sink):
    qw = GQA_G * HD_A
    kv_spec = pl.BlockSpec((SEQ, HD_A), lambda b, kv: (b, kv))
    return pl.pallas_call(
        _ctx_attn_kernel,
        grid=(BATCH, KV_A),
        in_specs=[pl.BlockSpec((SEQ, qw), lambda b, kv: (b, OFF_AQ // qw + kv)),
                  pl.BlockSpec((SEQ, HD_A), lambda b, kv: (b, OFF_AK // HD_A + kv)),
                  pl.BlockSpec((SEQ, HD_A), lambda b, kv: (b, OFF_AV // HD_A + kv)),
                  pl.BlockSpec((1, LANES), lambda b, kv: (0, 0))],
        out_specs=[pl.BlockSpec((SEQ, qw), lambda b, kv: (b, kv)), kv_spec, kv_spec],
        out_shape=[jax.ShapeDtypeStruct((M_CTX, BRANCH_W), BF16),
                   jax.ShapeDtypeStruct((M_CTX, KV_A * HD_A), F32),
                   jax.ShapeDtypeStruct((M_CTX, KV_A * HD_A), F32)],
        compiler_params=_params(("arbitrary", "arbitrary"), 32),
        name="attn_ctx",
    )(big, big, big, sink)


def _lat_attn_kernel(q_ref, k_ref, v_ref, kc_ref, vc_ref, cos_ref, sin_ref, sink_ref, o_ref, kr_s):
    kv = pl.program_id(1)
    scale = HD_A ** -0.5
    lane = lax.broadcasted_iota(jnp.int32, (1, HD_A), 1)
    nf = ROPE_AXIS // 2
    first_half = (lane % ROPE_AXIS) < nf

    def rope(x, sl):
        swapped = jnp.where(first_half, pltpu.roll(x, HD_A - nf, 1), pltpu.roll(x, nf, 1))
        return x * cos_ref[sl, :] + swapped * sin_ref[sl, :]

    kr_s[...] = rope(k_ref[...].astype(F32), slice(None)).astype(BF16)
    kcb = kc_ref[0, 0].astype(BF16)
    vcb = vc_ref[0, 0].astype(BF16)
    nblk = DEC_SEQ // QBLK

    def block_head(j, g):
        lo = max(j - 1, 0) * QBLK
        hi = min(j + 2, nblk) * QBLK
        qs = slice(j * QBLK, (j + 1) * QBLK)
        sl = slice(g * HD_A, (g + 1) * HD_A)
        qpos = j * QBLK + lax.broadcasted_iota(jnp.int32, (QBLK, hi - lo), 0)
        kpos = lo + lax.broadcasted_iota(jnp.int32, (QBLK, hi - lo), 1)
        mask = jnp.abs(kpos - qpos) <= WINDOW
        sk = _head_sink(sink_ref, kv * GQA_G + g)
        qg = rope(q_ref[qs, sl].astype(F32), qs).astype(BF16)
        s_band = jnp.where(mask, _dot_nt(qg, kr_s[lo:hi, :]) * scale, -jnp.inf)
        s_ctx = _dot_nt(qg, kcb) * scale
        yield
        m = jnp.maximum(jnp.maximum(jnp.max(s_band, axis=1, keepdims=True),
                                    jnp.max(s_ctx, axis=1, keepdims=True)), sk)
        p_band = jnp.exp(s_band - m)
        p_ctx = jnp.exp(s_ctx - m)
        yield
        den = (jnp.sum(p_band, axis=1, keepdims=True) + jnp.sum(p_ctx, axis=1, keepdims=True)
               + jnp.exp(sk - m))
        o = _dot(p_band.astype(BF16), v_ref[lo:hi, :]) + _dot(p_ctx.astype(BF16), vcb)
        yield
        o_ref[qs, sl] = (o / den).astype(o_ref.dtype)

    for j in range(0, nblk, 2):
        _run_interleaved([block_head(jj, g) for jj in (j, j + 1) for g in range(GQA_G)])


def _lat_attention(big, cache_k, cache_v, cos_t, sin_t, sink, layer):
    qw = GQA_G * HD_A
    r0 = M_CTX // DEC_SEQ
    return pl.pallas_call(
        _lat_attn_kernel,
        grid=(DEC_BATCH, KV_A),
        in_specs=[pl.BlockSpec((DEC_SEQ, qw), lambda b, kv: (r0 + b, OFF_AQ // qw + kv)),
                  pl.BlockSpec((DEC_SEQ, HD_A), lambda b, kv: (r0 + b, OFF_AK // HD_A + kv)),
                  pl.BlockSpec((DEC_SEQ, HD_A), lambda b, kv: (r0 + b, OFF_AV // HD_A + kv)),
                  pl.BlockSpec((1, 1, PAST_LEN, HD_A), lambda b, kv: (b, layer, 0, kv)),
                  pl.BlockSpec((1, 1, PAST_LEN, HD_A), lambda b, kv: (b, layer, 0, kv)),
                  pl.BlockSpec((DEC_SEQ, HD_A), lambda b, kv: (0, 0)),
                  pl.BlockSpec((DEC_SEQ, HD_A), lambda b, kv: (0, 0)),
                  pl.BlockSpec((1, LANES), lambda b, kv: (0, 0))],
        out_specs=pl.BlockSpec((DEC_SEQ, qw), lambda b, kv: (b, kv)),
        out_shape=jax.ShapeDtypeStruct((M_LAT, BRANCH_W), BF16),
        scratch_shapes=[pltpu.VMEM((DEC_SEQ, HD_A), BF16)],
        compiler_params=_params(("arbitrary", "arbitrary"), 32),
        name="attn_lat",
    )(big, big, big, cache_k, cache_v, cos_t, sin_t, sink)


def _rope_tables():
    rows = DEC_SEQ // GRID_W
    row = jnp.repeat(jnp.arange(rows, dtype=F32), GRID_W)
    col = jnp.tile(jnp.arange(GRID_W, dtype=F32), rows)
    nf = ROPE_AXIS // 2
    inv = ROPE_BASE ** (-jnp.arange(nf, dtype=F32) / nf)
    ar = row[:, None] * inv[None, :]
    ac = col[:, None] * inv[None, :]
    cr, sr, cc, sc = jnp.cos(ar), jnp.sin(ar), jnp.cos(ac), jnp.sin(ac)
    cos_t = jnp.concatenate([cr, cr, cc, cc], axis=1)
    sin_t = jnp.concatenate([-sr, sr, -sc, sc], axis=1)
    return cos_t, sin_t


def _pad_cols(a, width):
    return jnp.pad(a, ((0, 0), (0, width - a.shape[1])))


def kernel(x_prompt, x_sample, cache_k, cache_v, state_C, state_n, state_m, c, c_ctx, ada_w, ada_b, norm1_g, w_in, m_gate_b, m_norm_g, sgu_norm_g, sgu_w, sgu_b, attn_sink, w_br_m, w_br_s, w_br_a, w_out, norm2_g, ffn_up, ffn_conv_w, ffn_conv_b, ffn_down, final_g):
    cs = jnp.concatenate([c, c_ctx[None, :], jnp.zeros((MOD_ROWS - DEC_BATCH - 1, D_MODEL), F32)], axis=0)
    mod = _modulation(cs, ada_w, ada_b)
    w_down = _cast_pad_down(ffn_down)
    w_in_t = jnp.swapaxes(w_in, 1, 2)

    def pad_ffn(a):
        return jnp.pad(a, ((0, 0), (0, 0), (0, FFN_PAD - FFN_DIM)))

    conv_b = ffn_conv_b.reshape(DEPTH, 1, 2 * FFN_DIM)
    cwg, cwu = pad_ffn(ffn_conv_w[:, :, :FFN_DIM]), pad_ffn(ffn_conv_w[:, :, FFN_DIM:])
    cbg, cbu = pad_ffn(conv_b[:, :, :FFN_DIM]), pad_ffn(conv_b[:, :, FFN_DIM:])

    cos_t, sin_t = _rope_tables()
    cache_k2 = cache_k.reshape(DEC_BATCH, DEPTH, PAST_LEN, KV_A * HD_A)
    cache_v2 = cache_v.reshape(DEC_BATCH, DEPTH, PAST_LEN, KV_A * HD_A)
    st_n = state_n.reshape(DEC_BATCH, DEPTH, 2, H_M, 1, DK_M)
    st_m = jnp.broadcast_to(state_m[..., None, None], (DEC_BATCH, DEPTH, 2, H_M, 1, LANES))

    xs = (x_prompt.reshape(M_CTX, D_MODEL), x_sample.reshape(M_LAT, D_MODEL))
    ks, vs, cfs, nfs, mfs = [], [], [], [], []
    for l in range(DEPTH):
        sh1, sc1, g1, sh2, sc2, g2 = [mod[l, :, j * D_MODEL:(j + 1) * D_MODEL].reshape(MOD_ROWS, 1, D_MODEL)
                                      for j in range(6)]
        sink = _pad_cols(attn_sink[l].reshape(1, H_A), LANES)
        b_mg = m_gate_b[l].reshape(N_MG, 1)

        h = _norm_mod(xs, norm1_g[l], sc1, sh1)
        big = _in_proj(h, w_in_t, l)
        gpt = _gate_proj(h, w_in_t, b_mg, l)

        hm_c, cf, nf, mf = _mlstm(big, gpt, m_norm_g[l], l, latent=False)
        (hm_l,) = _mlstm(big, gpt, m_norm_g[l], l, latent=True, state=(state_C, st_n, st_m))
        hs = _sgu(big, sgu_norm_g[l], sgu_w[l], sgu_b[l])
        ha_c, k_new, v_new = _ctx_attention(big, sink)
        ha_l = _lat_attention(big, cache_k2, cache_v2, cos_t, sin_t, sink, l)

        y = _merge_proj(hm_c, hm_l, hs, ha_c, ha_l, w_br_m, w_br_s, w_br_a, big, l)
        x = _out_proj(y, w_out, l, xs, g1)

        h2 = _norm_mod((x,), norm2_g[l], sc2, sh2)
        u = _ffn_up(h2, ffn_up, cwg, cwu, cbg, cbu, l)
        x = _ffn_down(u, w_down, l, x, g2)
        xs = (x,)

        ks.append(k_new.reshape(BATCH, SEQ, KV_A, HD_A))
        vs.append(v_new.reshape(BATCH, SEQ, KV_A, HD_A))
        cfs.append(cf)
        nfs.append(nf.reshape(BATCH, 2, H_M, DK_M))
        mfs.append(mf[:, :, :, 0, 0])

    y_prompt = _final_norm(x, final_g, 0, M_CTX).reshape(BATCH, SEQ, D_MODEL)
    y_sample = _final_norm(x, final_g, M_CTX, M_LAT).reshape(DEC_BATCH, DEC_SEQ, D_MODEL)
    return (y_prompt, y_sample, jnp.stack(ks, axis=1), jnp.stack(vs, axis=1),
            jnp.stack(cfs, axis=1), jnp.stack(nfs, axis=1), jnp.stack(mfs, axis=1))
```
